```python
import math
import jax, jax.numpy as jnp
from jax import lax
import numpy as np

D_MODEL = 1024
BATCH = 16
SEQ = 4096
DEPTH = 4
DEC_BATCH = 8
DEC_SEQ = 8192
PAST_LEN = 128

A_HEADS = 4
A_QK_DIM = 64
A_V_DIM = 2 * A_QK_DIM
A_WIDTH = A_HEADS * A_V_DIM
Q_BLOCK = 128
B_GROUPS = 4
B_WIDTH = 256
B_GROUP_DIM = B_WIDTH // B_GROUPS
B_CHUNK = 128
C_HEADS = 4
C_WIDTH = 256
C_V_DIM = C_WIDTH // C_HEADS
C_K_DIM = C_V_DIM // 2
C_KEY_WIDTH = C_HEADS * C_K_DIM
C_DECAY_RANK = 16
C_GATE_NORMALIZER = 16.0
C_CHUNK = 64
MIX_WIDTH = A_WIDTH + B_WIDTH + C_WIDTH
PROJ_SIZES = (
    A_HEADS * 2 * A_QK_DIM,
    A_HEADS * 2 * A_QK_DIM,
    A_WIDTH,
    B_WIDTH,
    B_WIDTH,
    C_KEY_WIDTH,
    C_KEY_WIDTH,
    C_WIDTH,
    C_WIDTH,
    2 * C_DECAY_RANK,
)
PROJ_WIDTH = sum(PROJ_SIZES)
N_EXPERTS = 16
EC_CAPACITY_FACTOR = 2
EXPERT_FF = 2048
EPS = 1e-6

kernel_name = "hybrid_bidir_diffattn_gmlp_gla_ec"


def rmsnorm(x, g):
    xf = x.astype(jnp.float32)
    y = xf * lax.rsqrt(jnp.mean(xf * xf, axis=-1, keepdims=True) + EPS)
    return (y * g.astype(jnp.float32)).astype(x.dtype)


def layernorm(x, g, b):
    xf = x.astype(jnp.float32)
    mu = jnp.mean(xf, axis=-1, keepdims=True)
    var = jnp.mean(jnp.square(xf - mu), axis=-1, keepdims=True)
    y = (xf - mu) * lax.rsqrt(var + EPS) * g.astype(jnp.float32) + b.astype(jnp.float32)
    return y.astype(x.dtype)


def alibi_slopes(n):
    return jnp.array([2.0 ** (-8.0 * (i + 1) / n) for i in range(n)], dtype=jnp.float32)


def lambda_init_fn(layer_idx):
    return 0.8 - 0.6 * math.exp(-0.3 * layer_idx)


def diff_attention(q, k, v, lam, subln_g, lam_init):
    Bn, S = q.shape[0], q.shape[1]
    nblk = S // Q_BLOCK
    slopes = alibi_slopes(A_HEADS)
    kpos = jnp.arange(S)
    scale = A_QK_DIM ** -0.5
    qb = q.reshape(Bn, nblk, Q_BLOCK, A_HEADS, 2, A_QK_DIM).transpose(1, 0, 2, 3, 4, 5)

    def block(args):
        q_blk, start = args
        s = jnp.einsum('bqhmd,bkhmd->bhmqk', q_blk, k).astype(jnp.float32) * scale
        qpos = start + jnp.arange(Q_BLOCK)
        dist = jnp.abs(qpos[:, None] - kpos[None, :]).astype(jnp.float32)
        s = s - slopes[None, :, None, None, None] * dist[None, None, None]
        p = jax.nn.softmax(s, axis=-1)
        w = (p[:, :, 0] - lam * p[:, :, 1]).astype(v.dtype)
        return jnp.einsum('bhqk,bkhd->bqhd', w, v)

    o = lax.map(block, (qb, jnp.arange(nblk) * Q_BLOCK))
    o = o.transpose(1, 0, 2, 3, 4).reshape(Bn, S, A_HEADS, A_V_DIM)
    o = rmsnorm(o, subln_g) * (1.0 - lam_init)
    return o.reshape(Bn, S, A_WIDTH)


def spatial_gating(u, v, ln_g, ln_b, w_s, b_s):
    u = jax.nn.gelu(u)
    v = layernorm(jax.nn.gelu(v), ln_g, ln_b)
    Bn, S = v.shape[0], v.shape[1]
    vc = v.reshape(Bn, S // B_CHUNK, B_CHUNK, B_GROUPS, B_GROUP_DIM)
    mixed = jnp.einsum('gts,bcsgd->bctgd', w_s, vc) + b_s.T[:, :, None]
    return u * mixed.reshape(Bn, S, B_WIDTH)


def gla_direction(q, k, v, log_a):
    Bn, S, H, dk = q.shape
    dv = v.shape[-1]
    n = S // C_CHUNK
    mask = jnp.tril(jnp.ones((C_CHUNK, C_CHUNK), dtype=bool))

    def to_chunks(t):
        return t.reshape(Bn, n, C_CHUNK, H, t.shape[-1]).transpose(1, 0, 3, 2, 4)

    def step(state, inp):
        qc, kc, vc, lac = inp
        b = jnp.cumsum(lac, axis=2)
        o_inter = jnp.einsum('bhik,bhkv->bhiv', qc * jnp.exp(b), state)
        diff = b[:, :, :, None, :] - b[:, :, None, :, :]
        decay = jnp.exp(jnp.where(mask[:, :, None], diff, -jnp.inf))
        attn = jnp.einsum('bhik,bhjk,bhijk->bhij', qc, kc, decay)
        o_intra = jnp.einsum('bhij,bhjv->bhiv', attn, vc)
        b_last = b[:, :, -1:, :]
        k_dec = kc * jnp.exp(b_last - b)
        new_state = jnp.exp(b_last[:, :, 0, :])[..., None] * state + jnp.einsum('bhjk,bhjv->bhkv', k_dec, vc)
        return new_state, o_inter + o_intra

    state0 = jnp.zeros((Bn, H, dk, dv), jnp.float32)
    _, o = lax.scan(step, state0, (to_chunks(q), to_chunks(k), to_chunks(v), to_chunks(log_a)))
    return o.transpose(1, 0, 3, 2, 4).reshape(Bn, S, H, dv)


def gla_mixer(q, k, v, g, z, decay_w, decay_b, norm_g):
    Bn, S = q.shape[0], q.shape[1]
    f32 = jnp.float32
    qh = q.astype(f32).reshape(Bn, S, C_HEADS, C_K_DIM) * (C_K_DIM ** -0.5)
    kh = k.astype(f32).reshape(Bn, S, C_HEADS, C_K_DIM)
    vh = v.astype(f32).reshape(Bn, S, C_HEADS, C_V_DIM)
    zr = z.astype(f32).reshape(Bn, S, 2, C_DECAY_RANK)
    log_a = jax.nn.log_sigmoid(jnp.einsum('bsdr,drk->bsdk', zr, decay_w.astype(f32)) + decay_b.astype(f32)) / C_GATE_NORMALIZER
    log_a = log_a.reshape(Bn, S, 2, C_HEADS, C_K_DIM)
    o_fwd = gla_direction(qh, kh, vh, log_a[:, :, 0])
    flip = lambda t: jnp.flip(t, axis=1)
    o_bwd = flip(gla_direction(flip(qh), flip(kh), flip(vh), flip(log_a[:, :, 1])))
    o = rmsnorm(o_fwd + o_bwd, norm_g).reshape(Bn, S, C_WIDTH)
    return (o * jax.nn.silu(g.astype(f32))).astype(g.dtype)


def expert_choice_ffn(h, w_router, w_gate, w_up, w_down):
    Bn, S, D = h.shape
    T = Bn * S
    cap = EC_CAPACITY_FACTOR * T // N_EXPERTS
    xt = h.reshape(T, D)
    aff = jax.nn.softmax((xt @ w_router).astype(jnp.float32), axis=-1)
    gate, idx = lax.top_k(aff.T, cap)
    xe = xt[idx]
    hid = jax.nn.silu(jnp.einsum('ecd,edf->ecf', xe, w_gate)) * jnp.einsum('ecd,edf->ecf', xe, w_up)
    ye = jnp.einsum('ecf,efd->ecd', hid, w_down) * gate[..., None].astype(h.dtype)
    out = jnp.zeros_like(xt).at[idx.reshape(-1)].add(ye.reshape(-1, D))
    return out.reshape(Bn, S, D)


def trunk(x, norm1, w_in, lam_q1, lam_k1, lam_q2, lam_k2, diff_subln, ln_v_g, ln_v_b,
          w_spatial, b_spatial, gla_decay_w, gla_decay_b, gla_norm, w_out, norm2,
          w_router, w_gate, w_up, w_down, norm_f):
    Bn, S = x.shape[0], x.shape[1]
    offsets = [int(o) for o in np.cumsum(PROJ_SIZES)[:-1]]
    for l in range(DEPTH):
        hn = rmsnorm(x, norm1[l])
        proj = hn @ w_in[l]
        (aq, ak, av, bu, bv, cq, ck, cv, cg, cz) = jnp.split(proj, offsets, axis=-1)
        lam_init = lambda_init_fn(l)
        lam = (jnp.exp(jnp.sum(lam_q1[l].astype(jnp.float32) * lam_k1[l].astype(jnp.float32)))
               - jnp.exp(jnp.sum(lam_q2[l].astype(jnp.float32) * lam_k2[l].astype(jnp.float32)))
               + lam_init)
        o_a = diff_attention(aq.reshape(Bn, S, A_HEADS, 2, A_QK_DIM),
                             ak.reshape(Bn, S, A_HEADS, 2, A_QK_DIM),
                             av.reshape(Bn, S, A_HEADS, A_V_DIM),
                             lam, diff_subln[l], lam_init)
        o_b = spatial_gating(bu, bv, ln_v_g[l], ln_v_b[l], w_spatial[l], b_spatial[l])
        o_c = gla_mixer(cq, ck, cv, cg, cz, gla_decay_w[l], gla_decay_b[l], gla_norm[l])
        x = x + jnp.concatenate([o_a, o_b, o_c], axis=-1) @ w_out[l]
        x = x + expert_choice_ffn(rmsnorm(x, norm2[l]), w_router[l], w_gate[l], w_up[l], w_down[l])
    return rmsnorm(x, norm_f)


def setup_inputs(seed: int = 0) -> dict:
    key = jax.random.key(seed)
    ks = jax.random.split(key, 24)
    f32 = jnp.float32
    nrm = lambda k, shape, s: jax.random.normal(k, shape, f32) * s
    return {
        "x_prompt": nrm(ks[0], (BATCH, SEQ, D_MODEL), 1.0),
        "x_sample": nrm(ks[1], (DEC_BATCH, DEC_SEQ, D_MODEL), 1.0),
        "norm1": 1.0 + nrm(ks[2], (DEPTH, D_MODEL), 0.02),
        "w_in": nrm(ks[3], (DEPTH, D_MODEL, PROJ_WIDTH), D_MODEL ** -0.5),
        "lam_q1": nrm(ks[4], (DEPTH, A_QK_DIM), 0.1),
        "lam_k1": nrm(ks[5], (DEPTH, A_QK_DIM), 0.1),
        "lam_q2": nrm(ks[6], (DEPTH, A_QK_DIM), 0.1),
        "lam_k2": nrm(ks[7], (DEPTH, A_QK_DIM), 0.1),
        "diff_subln": 1.0 + nrm(ks[8], (DEPTH, A_V_DIM), 0.02),
        "ln_v_g": 1.0 + nrm(ks[9], (DEPTH, B_WIDTH), 0.02),
        "ln_v_b": nrm(ks[10], (DEPTH, B_WIDTH), 0.02),
        "w_spatial": nrm(ks[11], (DEPTH, B_GROUPS, B_CHUNK, B_CHUNK), B_CHUNK ** -0.5),
        "b_spatial": 1.0 + nrm(ks[12], (DEPTH, B_GROUPS, B_CHUNK), 0.02),
        "gla_decay_w": nrm(ks[13], (DEPTH, 2, C_DECAY_RANK, C_KEY_WIDTH), C_DECAY_RANK ** -0.5),
        "gla_decay_b": nrm(ks[14], (DEPTH, 2, C_KEY_WIDTH), 0.1),
        "gla_norm": 1.0 + nrm(ks[15], (DEPTH, C_V_DIM), 0.02),
        "w_out": nrm(ks[16], (DEPTH, MIX_WIDTH, D_MODEL), MIX_WIDTH ** -0.5),
        "norm2": 1.0 + nrm(ks[17], (DEPTH, D_MODEL), 0.02),
        "w_router": nrm(ks[18], (DEPTH, D_MODEL, N_EXPERTS), D_MODEL ** -0.5),
        "w_gate": nrm(ks[19], (DEPTH, N_EXPERTS, D_MODEL, EXPERT_FF), D_MODEL ** -0.5),
        "w_up": nrm(ks[20], (DEPTH, N_EXPERTS, D_MODEL, EXPERT_FF), D_MODEL ** -0.5),
        "w_down": nrm(ks[21], (DEPTH, N_EXPERTS, EXPERT_FF, D_MODEL), EXPERT_FF ** -0.5),
        "norm_f": 1.0 + nrm(ks[22], (D_MODEL,), 0.02),
    }


def reference(x_prompt, x_sample, norm1, w_in, lam_q1, lam_k1, lam_q2, lam_k2, diff_subln,
              ln_v_g, ln_v_b, w_spatial, b_spatial, gla_decay_w, gla_decay_b, gla_norm,
              w_out, norm2, w_router, w_gate, w_up, w_down, norm_f):
    y_prompt = trunk(x_prompt, norm1, w_in, lam_q1, lam_k1, lam_q2, lam_k2, diff_subln, ln_v_g, ln_v_b,
                     w_spatial, b_spatial, gla_decay_w, gla_decay_b, gla_norm, w_out, norm2,
                     w_router, w_gate, w_up, w_down, norm_f)
    y_sample = trunk(x_sample, norm1, w_in, lam_q1, lam_k1, lam_q2, lam_k2, diff_subln, ln_v_g, ln_v_b,
                     w_spatial, b_spatial, gla_decay_w, gla_decay_b, gla_norm, w_out, norm2,
                     w_router, w_gate, w_up, w_down, norm_f)
    return (y_prompt, y_sample)
```

```python
import functools
import math

import jax
import jax.numpy as jnp
from jax import lax
from jax.experimental import pallas as pl
from jax.experimental.pallas import tpu as pltpu

F32 = jnp.float32
BF16 = jnp.bfloat16

EPS = 1e-6
LOG2E = 1.4426950408889634

A_HEADS = 4
A_QK_DIM = 64
A_V_DIM = 128
A_WIDTH = A_HEADS * A_V_DIM
B_GROUPS = 4
B_WIDTH = 256
B_GROUP_DIM = 64
B_CHUNK = 128
C_HEADS = 4
C_WIDTH = 256
C_V_DIM = 64
C_K_DIM = 32
C_KEY_WIDTH = 128
C_DECAY_RANK = 16
C_GATE_NORMALIZER = 16.0
C_CHUNK = 64
N_EXPERTS = 16
EC_CAPACITY_FACTOR = 2

OFF_AQ, OFF_AK, OFF_AV = 0, 512, 1024
OFF_BU, OFF_BV = 1536, 1792
OFF_CQ, OFF_CK, OFF_CV, OFF_CG, OFF_CZ = 2048, 2176, 2304, 2560, 2816
PROJ_WIDTH = 2848
PROJ_PAD = 2944

VMEM_LIMIT = 56 * 1024 * 1024


def _cparams(sem):
    return pltpu.CompilerParams(dimension_semantics=sem, vmem_limit_bytes=VMEM_LIMIT)


def _gelu_tanh(x):
    return 0.5 * x * (1.0 + jnp.tanh(0.7978845608028654 * (x + 0.044715 * x * x * x)))


def _log_sigmoid(x):
    return jnp.minimum(x, 0.0) - jnp.log(1.0 + jnp.exp(-jnp.abs(x)))


def _inproj_kernel(x_ref, g1_ref, w_ref, lng_ref, lnb_ref, ws_ref, bs_ref, wdec_ref, bdec_ref,
                   aq_ref, ak_ref, av_ref, ob_ref, cq_ref, ck_ref, cv_ref, cg_ref, laf_ref, lab_ref):
    x = x_ref[...]
    ms = jnp.mean(x * x, axis=-1, keepdims=True)
    hn = (x * lax.rsqrt(ms + EPS) * g1_ref[...]).astype(BF16)
    proj = jnp.dot(hn, w_ref[...], preferred_element_type=F32)

    aq_ref[...] = (proj[:, OFF_AQ:OFF_AK] * (A_QK_DIM ** -0.5 * LOG2E)).astype(BF16)
    ak_ref[...] = proj[:, OFF_AK:OFF_AV].astype(BF16)
    av_ref[...] = proj[:, OFF_AV:OFF_BU].astype(BF16)

    u = _gelu_tanh(proj[:, OFF_BU:OFF_BV])
    v = _gelu_tanh(proj[:, OFF_BV:OFF_CQ])
    mu = jnp.mean(v, axis=-1, keepdims=True)
    var = jnp.mean(jnp.square(v - mu), axis=-1, keepdims=True)
    v = ((v - mu) * lax.rsqrt(var + EPS) * lng_ref[...] + lnb_ref[...]).astype(BF16)
    tm = x.shape[0]
    lane = lax.broadcasted_iota(jnp.int32, (B_CHUNK, 128), 1)
    first_half = lane < B_GROUP_DIM
    for c in range(tm // B_CHUNK):
        rows = slice(c * B_CHUNK, (c + 1) * B_CHUNK)
        parts = []
        for p in range(B_GROUPS // 2):
            vch = v[rows, p * 128:(p + 1) * 128]
            m0 = jnp.dot(ws_ref[2 * p], vch, preferred_element_type=F32)
            m1 = jnp.dot(ws_ref[2 * p + 1], vch, preferred_element_type=F32)
            parts.append(jnp.where(first_half, m0, m1))
        mixed = jnp.concatenate(parts, axis=1) + bs_ref[...]
        ob_ref[rows, :] = (u[rows, :] * mixed).astype(BF16)

    cq_ref[...] = proj[:, OFF_CQ:OFF_CK] * (C_K_DIM ** -0.5)
    ck_ref[...] = proj[:, OFF_CK:OFF_CV]
    cv_ref[...] = proj[:, OFF_CV:OFF_CG].astype(BF16)
    cg_ref[...] = proj[:, OFF_CG:OFF_CZ]
    z = proj[:, OFF_CZ:PROJ_PAD].astype(BF16)
    xd = jnp.dot(z, wdec_ref[...], preferred_element_type=F32) + bdec_ref[...]
    la = _log_sigmoid(xd) * (1.0 / C_GATE_NORMALIZER)
    laf_ref[...] = la[:, :C_KEY_WIDTH]
    lab_ref[...] = la[:, C_KEY_WIDTH:]


def _inproj(x2d, g1, w, lng, lnb, ws, bs, wdec, bdec, tm):
    T, D = x2d.shape
    row = lambda n: pl.BlockSpec((tm, n), lambda i: (i, 0))
    full = lambda a: pl.BlockSpec(a.shape, lambda i: (0,) * a.ndim)
    outs = [
        (A_WIDTH, BF16), (A_WIDTH, BF16), (A_WIDTH, BF16), (B_WIDTH, BF16),
        (C_KEY_WIDTH, F32), (C_KEY_WIDTH, F32), (C_WIDTH, BF16), (C_WIDTH, F32),
        (C_KEY_WIDTH, F32), (C_KEY_WIDTH, F32),
    ]
    return pl.pallas_call(
        _inproj_kernel,
        grid=(T // tm,),
        in_specs=[row(D), full(g1), full(w), full(lng), full(lnb), full(ws), full(bs), full(wdec), full(bdec)],
        out_specs=[row(n) for n, _ in outs],
        out_shape=[jax.ShapeDtypeStruct((T, n), dt) for n, dt in outs],
        compiler_params=_cparams(("parallel",)),
        name="inproj",
    )(x2d, g1, w, lng, lnb, ws, bs, wdec, bdec)


def _attn_kernel(lq1_ref, lk1_ref, lq2_ref, lk2_ref, g_ref, q_ref, k_ref, v_ref, o_ref,
                 *, tq, tk, lam_init):
    h = pl.program_id(1)
    qi = pl.program_id(2)
    S = k_ref.shape[1]
    lam = (jnp.exp(jnp.sum(lq1_ref[...] * lk1_ref[...], axis=-1, keepdims=True))
           - jnp.exp(jnp.sum(lq2_ref[...] * lk2_ref[...], axis=-1, keepdims=True)) + lam_init)

    q = q_ref[0]
    lane = lax.broadcasted_iota(jnp.int32, (tq, 128), 1)
    qm = (jnp.where(lane < A_QK_DIM, q, jnp.zeros_like(q)), jnp.where(lane >= A_QK_DIM, q, jnp.zeros_like(q)))
    slope2 = jnp.exp2(-8.0 * (h + 1).astype(F32) / A_HEADS) * LOG2E
    rel = (lax.broadcasted_iota(jnp.int32, (tq, tk), 0) - lax.broadcasted_iota(jnp.int32, (tq, tk), 1)).astype(F32)
    q0pos = (qi * tq).astype(F32)

    def body(j, carry):
        kj = k_ref[0, pl.ds(pl.multiple_of(j * tk, tk), tk), :]
        vj = v_ref[0, pl.ds(pl.multiple_of(j * tk, tk), tk), :]
        bias = slope2 * jnp.abs(rel + (q0pos - (j * tk).astype(F32)))
        new = []
        for m in range(2):
            mx, l, acc = carry[m]
            s = lax.dot_general(qm[m], kj, (((1,), (1,)), ((), ())), preferred_element_type=F32) - bias
            mn = jnp.maximum(mx, jnp.max(s, axis=-1, keepdims=True))
            p = jnp.exp2(s - mn)
            alpha = jnp.exp2(mx - mn)
            l = alpha * l + jnp.sum(p, axis=-1, keepdims=True)
            acc = alpha * acc + jnp.dot(p.astype(BF16), vj, preferred_element_type=F32)
            new.append((mn, l, acc))
        return tuple(new)

    init = tuple((jnp.full((tq, 1), -jnp.inf, F32), jnp.zeros((tq, 1), F32), jnp.zeros((tq, A_V_DIM), F32))
                 for _ in range(2))
    (m0, l0, a0), (m1, l1, a1) = lax.fori_loop(0, S // tk, body, init)
    o = a0 / l0 - lam * (a1 / l1)
    ms = jnp.mean(o * o, axis=-1, keepdims=True)
    o = o * lax.rsqrt(ms + EPS) * g_ref[...] * (1.0 - lam_init)
    o_ref[0] = o.astype(o_ref.dtype)


def _attention(aq, ak, av, lq1, lk1, lq2, lk2, subln, lam_init, tq, tk):
    B, S, _ = aq.shape
    vec = pl.BlockSpec((1, A_QK_DIM), lambda b, h, i: (0, 0))
    kern = functools.partial(_attn_kernel, tq=tq, tk=tk, lam_init=lam_init)
    return pl.pallas_call(
        kern,
        grid=(B, A_HEADS, S // tq),
        in_specs=[vec, vec, vec, vec,
                  pl.BlockSpec((1, A_V_DIM), lambda b, h, i: (0, 0)),
                  pl.BlockSpec((1, tq, 128), lambda b, h, i: (b, i, h)),
                  pl.BlockSpec((1, S, 128), lambda b, h, i: (b, 0, h)),
                  pl.BlockSpec((1, S, 128), lambda b, h, i: (b, 0, h))],
        out_specs=pl.BlockSpec((1, tq, 128), lambda b, h, i: (b, i, h)),
        out_shape=jax.ShapeDtypeStruct((B, S, A_WIDTH), BF16),
        compiler_params=_cparams(("parallel", "parallel", "arbitrary")),
        name="diff_attn",
    )(lq1, lk1, lq2, lk2, subln, aq, ak, av)


def _gla_chunk(q, k, v, la, st, reverse):
    C = C_CHUNK
    r = lax.broadcasted_iota(jnp.int32, (C, C), 0)
    c = lax.broadcasted_iota(jnp.int32, (C, C), 1)
    tri = (c >= r) if reverse else (c <= r)
    tri_b = tri.astype(BF16)
    la_hi = la.astype(BF16)
    la_lo = (la - la_hi.astype(F32)).astype(BF16)
    b = (jnp.dot(tri_b, la_hi, preferred_element_type=F32)
         + jnp.dot(tri_b, la_lo, preferred_element_type=F32))
    b_end = b[0:1, :] if reverse else b[C - 1:C, :]
    eb = jnp.exp(b)
    qt = (q * eb).astype(BF16)
    kt = k * jnp.exp(-b)
    kdec = (k * jnp.exp(b_end - b)).astype(BF16)

    lane_k = lax.broadcasted_iota(jnp.int32, (C, C_KEY_WIDTH), 1) // C_K_DIM
    kstack = jnp.concatenate([jnp.where(lane_k == hh, kt, 0.0) for hh in range(C_HEADS)], axis=0).astype(BF16)
    attn = lax.dot_general(qt, kstack, (((1,), (1,)), ((), ())), preferred_element_type=F32)
    ri = lax.broadcasted_iota(jnp.int32, (C, C_HEADS * C), 0)
    cj = lax.broadcasted_iota(jnp.int32, (C, C_HEADS * C), 1) % C
    keep = (cj >= ri) if reverse else (cj <= ri)
    attn = jnp.where(keep, attn, 0.0).astype(BF16)
    lane_v = lax.broadcasted_iota(jnp.int32, (C, C_WIDTH), 1) // C_V_DIM
    vstack = jnp.concatenate([jnp.where(lane_v == hh, v, jnp.zeros_like(v)) for hh in range(C_HEADS)], axis=0)
    o = jnp.dot(attn, vstack, preferred_element_type=F32)
    o = o + lax.dot_general(qt, st.astype(BF16), (((1,), (1,)), ((), ())), preferred_element_type=F32)

    upd = lax.dot_general(v, kdec, (((0,), (0,)), ((), ())), preferred_element_type=F32)
    rh = lax.broadcasted_iota(jnp.int32, (C_WIDTH, C_KEY_WIDTH), 0) // C_V_DIM
    ch = lax.broadcasted_iota(jnp.int32, (C_WIDTH, C_KEY_WIDTH), 1) // C_K_DIM
    st = st * jnp.exp(b_end) + jnp.where(rh == ch, upd, 0.0)
    return o, st


def _gla_fwd_kernel(q_ref, k_ref, v_ref, la_ref, o_ref, st_ref, *, nchunk):
    @pl.when(pl.program_id(1) == 0)
    def _():
        st_ref[...] = jnp.zeros_like(st_ref)

    st = st_ref[...]
    for c in range(nchunk):
        rows = slice(c * C_CHUNK, (c + 1) * C_CHUNK)
        o, st = _gla_chunk(q_ref[0, rows, :], k_ref[0, rows, :], v_ref[0, rows, :], la_ref[0, rows, :], st, False)
        o_ref[0, rows, :] = o
    st_ref[...] = st


def _gla_bwd_kernel(q_ref, k_ref, v_ref, la_ref, of_ref, g_ref, gn_ref, o_ref, st_ref, *, nchunk):
    @pl.when(pl.program_id(1) == 0)
    def _():
        st_ref[...] = jnp.zeros_like(st_ref)

    st = st_ref[...]
    lane_h = lax.broadcasted_iota(jnp.int32, (C_WIDTH, C_WIDTH), 0) // C_V_DIM
    lane_h2 = lax.broadcasted_iota(jnp.int32, (C_WIDTH, C_WIDTH), 1) // C_V_DIM
    seg = (lane_h == lane_h2).astype(F32) * (1.0 / C_V_DIM)
    for c in reversed(range(nchunk)):
        rows = slice(c * C_CHUNK, (c + 1) * C_CHUNK)
        o, st = _gla_chunk(q_ref[0, rows, :], k_ref[0, rows, :], v_ref[0, rows, :], la_ref[0, rows, :], st, True)
        o = o + of_ref[0, rows, :]
        sq = o * o
        sq_hi = sq.astype(BF16)
        sq_lo = (sq - sq_hi.astype(F32)).astype(BF16)
        segb = seg.astype(BF16)
        ms = (jnp.dot(sq_hi, segb, preferred_element_type=F32) + jnp.dot(sq_lo, segb, preferred_element_type=F32))
        o = o * lax.rsqrt(ms + EPS) * gn_ref[...]
        g = g_ref[0, rows, :]
        o_ref[0, rows, :] = (o * (g * jax.nn.sigmoid(g))).astype(o_ref.dtype)
    st_ref[...] = st


def _gla(cq, ck, cv, cg, laf, lab, gn_tiled, rb):
    B, S, _ = cq.shape
    nb = S // rb
    nchunk = rb // C_CHUNK
    fspec = lambda n: pl.BlockSpec((1, rb, n), lambda b, i: (b, i, 0))
    bspec = lambda n: pl.BlockSpec((1, rb, n), lambda b, i: (b, nb - 1 - i, 0))
    st = pltpu.VMEM((C_WIDTH, C_KEY_WIDTH), F32)
    o_f = pl.pallas_call(
        functools.partial(_gla_fwd_kernel, nchunk=nchunk),
        grid=(B, nb),
        in_specs=[fspec(C_KEY_WIDTH), fspec(C_KEY_WIDTH), fspec(C_WIDTH), fspec(C_KEY_WIDTH)],
        out_specs=fspec(C_WIDTH),
        out_shape=jax.ShapeDtypeStruct((B, S, C_WIDTH), F32),
        scratch_shapes=[st],
        compiler_params=_cparams(("parallel", "arbitrary")),
        name="gla_fwd",
    )(cq, ck, cv, laf)
    return pl.pallas_call(
        functools.partial(_gla_bwd_kernel, nchunk=nchunk),
        grid=(B, nb),
        in_specs=[bspec(C_KEY_WIDTH), bspec(C_KEY_WIDTH), bspec(C_WIDTH), bspec(C_KEY_WIDTH),
                  bspec(C_WIDTH), bspec(C_WIDTH), pl.BlockSpec((1, C_WIDTH), lambda b, i: (0, 0))],
        out_specs=bspec(C_WIDTH),
        out_shape=jax.ShapeDtypeStruct((B, S, C_WIDTH), BF16),
        scratch_shapes=[st],
        compiler_params=_cparams(("parallel", "arbitrary")),
        name="gla_bwd",
    )(cq, ck, cv, lab, o_f, cg, gn_tiled)


def _outproj_kernel(x_ref, oa_ref, ob_ref, oc_ref, w_ref, g2_ref, wr_hi_ref, wr_lo_ref,
                    h_ref, hn_ref, aff_ref):
    mix = jnp.concatenate([oa_ref[...], ob_ref[...], oc_ref[...]], axis=1)
    h = x_ref[...] + jnp.dot(mix, w_ref[...], preferred_element_type=F32)
    h_ref[...] = h
    ms = jnp.mean(h * h, axis=-1, keepdims=True)
    hn = h * lax.rsqrt(ms + EPS) * g2_ref[...]
    hn_ref[...] = hn.astype(hn_ref.dtype)
    hn_hi = hn.astype(BF16)
    hn_lo = (hn - hn_hi.astype(F32)).astype(BF16)
    logits = (jnp.dot(hn_hi, wr_hi_ref[...], preferred_element_type=F32)
              + jnp.dot(hn_hi, wr_lo_ref[...], preferred_element_type=F32)
              + jnp.dot(hn_lo, wr_hi_ref[...], preferred_element_type=F32))
    lane = lax.broadcasted_iota(jnp.int32, logits.shape, 1)
    logits = jnp.where(lane < N_EXPERTS, logits, -jnp.inf)
    mx = jnp.max(logits, axis=-1, keepdims=True)
    e = jnp.exp(logits - mx)
    aff = e / jnp.sum(e, axis=-1, keepdims=True)
    aff_ref[...] = jnp.transpose(aff)[:N_EXPERTS, :]


def _outproj(x2d, oa, ob, oc, w, g2, wr_hi, wr_lo, tm):
    T, D = x2d.shape
    row = lambda n: pl.BlockSpec((tm, n), lambda i: (i, 0))
    full = lambda a: pl.BlockSpec(a.shape, lambda i: (0,) * a.ndim)
    return pl.pallas_call(
        _outproj_kernel,
        grid=(T // tm,),
        in_specs=[row(D), row(A_WIDTH), row(B_WIDTH), row(C_WIDTH), full(w), full(g2), full(wr_hi), full(wr_lo)],
        out_specs=[row(D), row(D), pl.BlockSpec((N_EXPERTS, tm), lambda i: (0, i))],
        out_shape=[jax.ShapeDtypeStruct((T, D), F32), jax.ShapeDtypeStruct((T, D), BF16),
                   jax.ShapeDtypeStruct((N_EXPERTS, T), F32)],
        compiler_params=_cparams(("parallel",)),
        name="outproj_router",
    )(x2d, oa, ob, oc, w, g2, wr_hi, wr_lo)


def _ffn_kernel(x_ref, gate_ref, wg_ref, wu_ref, wd_ref, o_ref, *, fc):
    x = x_ref[0]
    F = wg_ref.shape[2]
    acc = jnp.zeros((x.shape[0], o_ref.shape[2]), F32)
    for c in range(F // fc):
        cols = slice(c * fc, (c + 1) * fc)
        g = jnp.dot(x, wg_ref[0, :, cols], preferred_element_type=F32)
        u = jnp.dot(x, wu_ref[0, :, cols], preferred_element_type=F32)
        hid = (g * jax.nn.sigmoid(g) * u).astype(BF16)
        acc = acc + jnp.dot(hid, wd_ref[0, cols, :], preferred_element_type=F32)
    o_ref[0] = (acc * gate_ref[0]).astype(o_ref.dtype)


def _ffn(xe, gate, wg, wu, wd, tm, fc):
    E, cap, D = xe.shape
    F = wg.shape[2]
    fc = min(fc, F)
    return pl.pallas_call(
        functools.partial(_ffn_kernel, fc=fc),
        grid=(E, cap // tm),
        in_specs=[pl.BlockSpec((1, tm, D), lambda e, i: (e, i, 0)),
                  pl.BlockSpec((1, tm, 1), lambda e, i: (e, i, 0)),
                  pl.BlockSpec((1, D, F), lambda e, i: (e, 0, 0)),
                  pl.BlockSpec((1, D, F), lambda e, i: (e, 0, 0)),
                  pl.BlockSpec((1, F, D), lambda e, i: (e, 0, 0))],
        out_specs=pl.BlockSpec((1, tm, D), lambda e, i: (e, i, 0)),
        out_shape=jax.ShapeDtypeStruct((E, cap, D), F32),
        compiler_params=_cparams(("parallel", "arbitrary")),
        name="expert_ffn",
    )(xe, gate, wg, wu, wd)


def _final_norm_kernel(x_ref, g_ref, o_ref):
    x = x_ref[...]
    ms = jnp.mean(x * x, axis=-1, keepdims=True)
    o_ref[...] = x * lax.rsqrt(ms + EPS) * g_ref[...]


def _final_norm(x2d, g, tm):
    T, D = x2d.shape
    return pl.pallas_call(
        _final_norm_kernel,
        grid=(T // tm,),
        in_specs=[pl.BlockSpec((tm, D), lambda i: (i, 0)), pl.BlockSpec((1, D), lambda i: (0, 0))],
        out_specs=pl.BlockSpec((tm, D), lambda i: (i, 0)),
        out_shape=jax.ShapeDtypeStruct((T, D), F32),
        compiler_params=_cparams(("parallel",)),
        name="final_norm",
    )(x2d, g)


def _prep_layer(l, norm1, w_in, ln_v_g, ln_v_b, w_spatial, b_spatial, gla_decay_w, gla_decay_b, gla_norm,
                w_out, norm2, w_router, diff_subln):
    D = w_in.shape[1]
    w = jnp.pad(w_in[l], ((0, 0), (0, PROJ_PAD - PROJ_WIDTH))).astype(BF16)
    bs = jnp.repeat(b_spatial[l].T, B_GROUP_DIM, axis=1)
    wdec = jnp.zeros((128, 2 * C_KEY_WIDTH), F32)
    wdec = wdec.at[:C_DECAY_RANK, :C_KEY_WIDTH].set(gla_decay_w[l, 0])
    wdec = wdec.at[C_DECAY_RANK:2 * C_DECAY_RANK, C_KEY_WIDTH:].set(gla_decay_w[l, 1])
    bdec = gla_decay_b[l].reshape(1, 2 * C_KEY_WIDTH)
    wr = jnp.pad(w_router[l], ((0, 0), (0, 128 - N_EXPERTS)))
    wr_hi = wr.astype(BF16)
    wr_lo = (wr - wr_hi.astype(F32)).astype(BF16)
    return dict(
        g1=norm1[l].reshape(1, D), w=w, lng=ln_v_g[l].reshape(1, B_WIDTH), lnb=ln_v_b[l].reshape(1, B_WIDTH),
        ws=w_spatial[l].astype(BF16), bs=bs, wdec=wdec.astype(BF16), bdec=bdec,
        gn=jnp.tile(gla_norm[l], C_HEADS).reshape(1, C_WIDTH), w_out=w_out[l].astype(BF16),
        g2=norm2[l].reshape(1, D), wr_hi=wr_hi, wr_lo=wr_lo, subln=diff_subln[l].reshape(1, A_V_DIM),
    )


def _tile(n, pref):
    t = min(n, pref)
    while n % t:
        t //= 2
    return t


def _trunk(x, layers, lam_vecs, ffn_w, norm_f):
    B, S, D = x.shape
    T = B * S
    cap = EC_CAPACITY_FACTOR * T // N_EXPERTS
    tm = _tile(T, 512)
    x2d = x.reshape(T, D)
    for l, p in enumerate(layers):
        lam_init = 0.8 - 0.6 * math.exp(-0.3 * l)
        aq, ak, av, ob, cq, ck, cv, cg, laf, lab = _inproj(
            x2d, p["g1"], p["w"], p["lng"], p["lnb"], p["ws"], p["bs"], p["wdec"], p["bdec"], tm)
        r3 = lambda a: a.reshape(B, S, a.shape[-1])
        lq1, lk1, lq2, lk2 = (v[l:l + 1] for v in lam_vecs)
        oa = _attention(r3(aq), r3(ak), r3(av), lq1, lk1, lq2, lk2, p["subln"], lam_init,
                        _tile(S, 256), _tile(S, 512))
        oc = _gla(r3(cq), r3(ck), r3(cv), r3(cg), r3(laf), r3(lab), p["gn"], _tile(S, 512))
        h, hn, aff_t = _outproj(x2d, oa.reshape(T, A_WIDTH), ob, oc.reshape(T, C_WIDTH),
                                p["w_out"], p["g2"], p["wr_hi"], p["wr_lo"], tm)
        gate, idx = lax.top_k(aff_t, cap)
        xe = hn[idx]
        wg, wu, wd = (w[l] for w in ffn_w)
        ye = _ffn(xe, gate[..., None], wg, wu, wd, _tile(cap, 512), 512)
        x2d = h.at[idx.reshape(-1)].add(ye.reshape(-1, D))
    return _final_norm(x2d, norm_f.reshape(1, D), tm).reshape(B, S, D)


def kernel(x_prompt, x_sample, norm1, w_in, lam_q1, lam_k1, lam_q2, lam_k2, diff_subln, ln_v_g, ln_v_b,
           w_spatial, b_spatial, gla_decay_w, gla_decay_b, gla_norm, w_out, norm2, w_router, w_gate, w_up,
           w_down, norm_f):
    depth = w_in.shape[0]
    layers = [_prep_layer(l, norm1, w_in, ln_v_g, ln_v_b, w_spatial, b_spatial, gla_decay_w, gla_decay_b,
                          gla_norm, w_out, norm2, w_router, diff_subln) for l in range(depth)]
    ffn_w = (w_gate.astype(BF16), w_up.astype(BF16), w_down.astype(BF16))
    lam_vecs = (lam_q1, lam_k1, lam_q2, lam_k2)
    y_prompt = _trunk(x_prompt, layers, lam_vecs, ffn_w, norm_f)
    y_sample = _trunk(x_sample, layers, lam_vecs, ffn_w, norm_f)
    return (y_prompt, y_sample)
```

```python
import functools
import math

import jax
import jax.numpy as jnp
from jax import lax
from jax.experimental import pallas as pl
from jax.experimental.pallas import tpu as pltpu

F32 = jnp.float32
BF16 = jnp.bfloat16

EPS = 1e-6
LOG2E = 1.4426950408889634

A_HEADS = 4
A_QK_DIM = 64
A_V_DIM = 128
A_WIDTH = A_HEADS * A_V_DIM
B_GROUPS = 4
B_WIDTH = 256
B_GROUP_DIM = 64
B_CHUNK = 128
C_HEADS = 4
C_WIDTH = 256
C_V_DIM = 64
C_K_DIM = 32
C_KEY_WIDTH = 128
C_DECAY_RANK = 16
C_GATE_NORMALIZER = 16.0
C_CHUNK = 64
N_EXPERTS = 16
EC_CAPACITY_FACTOR = 2

OFF_AQ, OFF_AK, OFF_AV = 0, 512, 1024
OFF_BU, OFF_BV = 1536, 1792
OFF_CQ, OFF_CK, OFF_CV, OFF_CG, OFF_CZ = 2048, 2176, 2304, 2560, 2816
PROJ_WIDTH = 2848
PROJ_PAD = 2944

VMEM_LIMIT = 56 * 1024 * 1024


def _cparams(sem):
    return pltpu.CompilerParams(dimension_semantics=sem, vmem_limit_bytes=VMEM_LIMIT)


def _gelu_tanh(x):
    return 0.5 * x * (1.0 + jnp.tanh(0.7978845608028654 * (x + 0.044715 * x * x * x)))


def _log_sigmoid(x):
    return jnp.minimum(x, 0.0) - jnp.log(1.0 + jnp.exp(-jnp.abs(x)))


def _inproj_kernel(x_ref, g1_ref, w_ref, lng_ref, lnb_ref, ws_ref, bs_ref, wdec_ref, bdec_ref,
                   aq_ref, ak_ref, av_ref, ob_ref, cq_ref, ck_ref, cv_ref, cg_ref, laf_ref, lab_ref):
    x = x_ref[...]
    ms = jnp.mean(x * x, axis=-1, keepdims=True)
    hn = (x * lax.rsqrt(ms + EPS) * g1_ref[...]).astype(BF16)
    proj = jnp.dot(hn, w_ref[...], preferred_element_type=F32)

    aq_ref[...] = (proj[:, OFF_AQ:OFF_AK] * (A_QK_DIM ** -0.5 * LOG2E)).astype(BF16)
    ak_ref[...] = proj[:, OFF_AK:OFF_AV].astype(BF16)
    tkb = av_ref.shape[2]
    for c in range(av_ref.shape[0]):
        av_ref[c] = jnp.transpose(proj[c * tkb:(c + 1) * tkb, OFF_AV:OFF_BU]).astype(BF16)

    u = _gelu_tanh(proj[:, OFF_BU:OFF_BV])
    v = _gelu_tanh(proj[:, OFF_BV:OFF_CQ])
    mu = jnp.mean(v, axis=-1, keepdims=True)
    var = jnp.mean(jnp.square(v - mu), axis=-1, keepdims=True)
    v = ((v - mu) * lax.rsqrt(var + EPS) * lng_ref[...] + lnb_ref[...]).astype(BF16)
    tm = x.shape[0]
    lane = lax.broadcasted_iota(jnp.int32, (B_CHUNK, 128), 1)
    first_half = lane < B_GROUP_DIM
    for c in range(tm // B_CHUNK):
        rows = slice(c * B_CHUNK, (c + 1) * B_CHUNK)
        parts = []
        for p in range(B_GROUPS // 2):
            vch = v[rows, p * 128:(p + 1) * 128]
            m0 = jnp.dot(ws_ref[2 * p], vch, preferred_element_type=F32)
            m1 = jnp.dot(ws_ref[2 * p + 1], vch, preferred_element_type=F32)
            parts.append(jnp.where(first_half, m0, m1))
        mixed = jnp.concatenate(parts, axis=1) + bs_ref[...]
        ob_ref[rows, :] = (u[rows, :] * mixed).astype(BF16)

    cq_ref[...] = proj[:, OFF_CQ:OFF_CK] * (C_K_DIM ** -0.5)
    ck_ref[...] = proj[:, OFF_CK:OFF_CV]
    cv_ref[...] = proj[:, OFF_CV:OFF_CG].astype(BF16)
    cg_ref[...] = proj[:, OFF_CG:OFF_CZ]
    z = proj[:, OFF_CZ:PROJ_PAD].astype(BF16)
    xd = jnp.dot(z, wdec_ref[...], preferred_element_type=F32) + bdec_ref[...]
    la = _log_sigmoid(xd) * (1.0 / C_GATE_NORMALIZER)
    laf_ref[...] = la[:, :C_KEY_WIDTH]
    lab_ref[...] = la[:, C_KEY_WIDTH:]


def _inproj(x2d, g1, w, lng, lnb, ws, bs, wdec, bdec, tm, tk):
    T, D = x2d.shape
    row = lambda n: pl.BlockSpec((tm, n), lambda i: (i, 0))
    full = lambda a: pl.BlockSpec(a.shape, lambda i: (0,) * a.ndim)
    outs = [
        (A_WIDTH, BF16), (A_WIDTH, BF16), None, (B_WIDTH, BF16),
        (C_KEY_WIDTH, F32), (C_KEY_WIDTH, F32), (C_WIDTH, BF16), (C_WIDTH, F32),
        (C_KEY_WIDTH, F32), (C_KEY_WIDTH, F32),
    ]
    vt_spec = pl.BlockSpec((tm // tk, A_WIDTH, tk), lambda i: (i, 0, 0))
    vt_shape = jax.ShapeDtypeStruct((T // tk, A_WIDTH, tk), BF16)
    return pl.pallas_call(
        _inproj_kernel,
        grid=(T // tm,),
        in_specs=[row(D), full(g1), full(w), full(lng), full(lnb), full(ws), full(bs), full(wdec), full(bdec)],
        out_specs=[vt_spec if o is None else row(o[0]) for o in outs],
        out_shape=[vt_shape if o is None else jax.ShapeDtypeStruct((T, o[0]), o[1]) for o in outs],
        compiler_params=_cparams(("parallel",)),
        name="inproj",
    )(x2d, g1, w, lng, lnb, ws, bs, wdec, bdec)


N_POS_LANES = 9


def _split3(x):
    hi = x.astype(BF16)
    r1 = x - hi.astype(F32)
    mid = r1.astype(BF16)
    lo = (r1 - mid.astype(F32)).astype(BF16)
    return hi, mid, lo


def _attn_kernel(lq1_ref, lk1_ref, lq2_ref, lk2_ref, g_ref, kpos_ref, q_ref, k_ref, vt_ref, o_ref,
                 qall_ref, *chain_refs, tq, tk, nq, lam_init):
    nch = 2 * nq
    grp = lambda g: chain_refs[g * nch:(g + 1) * nch]
    m_refs, l_refs, acc_refs = grp(0), grp(1), grp(2)
    s_refs, p_refs, al_refs = (grp(3), grp(4)), (grp(5), grp(6)), (grp(7), grp(8))
    h = pl.program_id(1)
    qi = pl.program_id(2)
    nblk = k_ref.shape[1] // tk
    nrest = nblk - 1
    lam = (jnp.exp(jnp.sum(lq1_ref[...] * lk1_ref[...], axis=-1, keepdims=True))
           - jnp.exp(jnp.sum(lq2_ref[...] * lk2_ref[...], axis=-1, keepdims=True)) + lam_init)
    c = jnp.exp2(-8.0 * (h + 1).astype(F32) / A_HEADS) * LOG2E

    lane = lax.broadcasted_iota(jnp.int32, (tq, 128), 1)
    il = lax.broadcasted_iota(jnp.int32, (tq, 128), 0).astype(F32)
    hi, mid, lo = _split3(jnp.where(lane < 3, -c * il, c))
    piece = lane % 3
    qpos = jnp.where(piece == 0, hi, jnp.where(piece == 1, mid, lo))
    qpos = jnp.where(lane < N_POS_LANES, qpos, jnp.zeros_like(qpos))
    chains = [(t, m) for t in range(nq) for m in range(2)]
    for ci, (t, m) in enumerate(chains):
        q = q_ref[0, t * tq:(t + 1) * tq, :]
        keep = (lane < A_QK_DIM) if m == 0 else (lane >= A_QK_DIM)
        qm = jnp.where(keep, q, jnp.zeros_like(q))
        qall_ref[0, ci] = jnp.concatenate([qm, qpos], axis=1)
        qall_ref[1, ci] = jnp.concatenate([qm, -qpos], axis=1)
    kpos = kpos_ref[...]

    def block_of(i):
        above = (i >= qi).astype(jnp.int32)
        return i + above, above

    def scores(j, variant, ci):
        kaug = jnp.concatenate([k_ref[0, pl.ds(pl.multiple_of(j * tk, tk), tk), :], kpos], axis=1)
        return lax.dot_general(kaug, qall_ref[variant, ci], (((1,), (1,)), ((), ())),
                               preferred_element_type=F32)

    def softmax(j, ci, s, first):
        t, _ = chains[ci]
        off = ((qi * nq + t) * tq - j * tk).astype(F32)
        delta = c * jnp.abs(off)
        bmax = jnp.max(s, axis=0, keepdims=True) - delta
        if first:
            mn = bmax
        else:
            mx = m_refs[ci][...]
            mn = jnp.maximum(mx, bmax)
        p = jnp.exp2(s - (mn + delta))
        psum = jnp.sum(p, axis=0, keepdims=True)
        m_refs[ci][...] = mn
        if first:
            l_refs[ci][...] = psum
            return p.astype(BF16), None
        alpha = jnp.exp2(mx - mn)
        l_refs[ci][...] = alpha * l_refs[ci][...] + psum
        return p.astype(BF16), alpha

    for ci, (t, m) in enumerate(chains):
        s = scores(qi, 0, ci)
        rel = (lax.broadcasted_iota(jnp.int32, (tk, tq), 0)
               - lax.broadcasted_iota(jnp.int32, (tk, tq), 1)).astype(F32)
        s = s - (2.0 * c) * jnp.maximum(rel - float(t * tq), 0.0)
        p, _ = softmax(qi, ci, s, True)
        acc_refs[ci][...] = jnp.dot(vt_ref[qi], p, preferred_element_type=F32)

    def tick(tau, par, do_qk, do_sm, do_pv):
        if do_qk:
            jq, vq = block_of(tau)
        if do_sm:
            js, _ = block_of(tau - 1)
        if do_pv:
            jp, _ = block_of(tau - 2)
        for half in range(0, nch, 2):
            for ci in (half, half + 1):
                if do_qk:
                    s_refs[par][ci][...] = scores(jq, vq, ci)
            for ci in (half, half + 1):
                if do_sm:
                    p, alpha = softmax(js, ci, s_refs[1 - par][ci][...], False)
                    p_refs[1 - par][ci][...] = p
                    al_refs[1 - par][ci][...] = alpha
            for ci in (half, half + 1):
                if do_pv:
                    pv = jnp.dot(vt_ref[jp], p_refs[par][ci][...], preferred_element_type=F32)
                    acc_refs[ci][...] = al_refs[par][ci][...] * acc_refs[ci][...] + pv

    L = nrest
    if L > 0:
        assert L % 2 == 1
        tick(0, 0, True, False, False)
        tick(1, 1, L > 1, True, False)
        if L > 1:
            tick(2, 0, True, True, True)

            def pair(ip, carry):
                tau = 3 + 2 * ip
                tick(tau, 1, True, True, True)
                tick(tau + 1, 0, True, True, True)
                return carry

            lax.fori_loop(0, (L - 3) // 2, pair, 0)
            tick(L, 1, False, True, True)
        tick(L + 1, (L + 1) % 2, False, False, True)

    for t in range(nq):
        a0, a1 = acc_refs[2 * t][...], acc_refs[2 * t + 1][...]
        o = a0 / l_refs[2 * t][...] - lam * (a1 / l_refs[2 * t + 1][...])
        ms = jnp.mean(o * o, axis=0, keepdims=True)
        o = o * lax.rsqrt(ms + EPS) * g_ref[...] * (1.0 - lam_init)
        o_ref[0, t * tq:(t + 1) * tq, :] = jnp.transpose(o).astype(o_ref.dtype)


def _attention(aq, ak, avt, lq1, lk1, lq2, lk2, subln_col, kpos, lam_init, tq, tk):
    B, S, _ = aq.shape
    nq = tk // tq
    nch = 2 * nq
    assert S == tk or (S // tk) % 2 == 0
    vec = pl.BlockSpec((1, A_QK_DIM), lambda b, h, i: (0, 0))
    kern = functools.partial(_attn_kernel, tq=tq, tk=tk, nq=nq, lam_init=lam_init)
    return pl.pallas_call(
        kern,
        grid=(B, A_HEADS, S // tk),
        in_specs=[vec, vec, vec, vec,
                  pl.BlockSpec((A_V_DIM, 1), lambda b, h, i: (0, 0)),
                  pl.BlockSpec((tk, 128), lambda b, h, i: (0, 0)),
                  pl.BlockSpec((1, tk, 128), lambda b, h, i: (b, i, h)),
                  pl.BlockSpec((1, S, 128), lambda b, h, i: (b, 0, h)),
                  pl.BlockSpec((S // tk, A_V_DIM, tk), lambda b, h, i: (b, h, 0))],
        out_specs=pl.BlockSpec((1, tk, 128), lambda b, h, i: (b, i, h)),
        out_shape=jax.ShapeDtypeStruct((B, S, A_WIDTH), BF16),
        scratch_shapes=([pltpu.VMEM((2, nch, tq, 256), BF16)]
                        + [pltpu.VMEM((1, tq), F32)] * (2 * nch)
                        + [pltpu.VMEM((A_V_DIM, tq), F32)] * nch
                        + [pltpu.VMEM((tk, tq), F32)] * (2 * nch)
                        + [pltpu.VMEM((tk, tq), BF16)] * (2 * nch)
                        + [pltpu.VMEM((1, tq), F32)] * (2 * nch)),
        compiler_params=_cparams(("parallel", "parallel", "arbitrary")),
        name="diff_attn",
    )(lq1, lk1, lq2, lk2, subln_col, kpos, aq, ak, avt)


def _key_pos_lanes(tk):
    j = jnp.arange(tk, dtype=jnp.int32)
    jlo = (j % 256).astype(F32)
    jhi = (j - j % 256).astype(F32)
    cols = [jnp.ones((tk,), F32)] * 3 + [jlo] * 3 + [jhi] * 3
    kp = jnp.stack(cols, axis=1)
    return jnp.pad(kp, ((0, 0), (0, 128 - N_POS_LANES))).astype(BF16)


def _gla_chunk(q, k, v, la, st, reverse):
    C = C_CHUNK
    r = lax.broadcasted_iota(jnp.int32, (C, C), 0)
    c = lax.broadcasted_iota(jnp.int32, (C, C), 1)
    tri = (c >= r) if reverse else (c <= r)
    tri_b = tri.astype(BF16)
    la_hi = la.astype(BF16)
    la_lo = (la - la_hi.astype(F32)).astype(BF16)
    b = (jnp.dot(tri_b, la_hi, preferred_element_type=F32)
         + jnp.dot(tri_b, la_lo, preferred_element_type=F32))
    b_end = b[0:1, :] if reverse else b[C - 1:C, :]
    eb = jnp.exp(b)
    qt = (q * eb).astype(BF16)
    kt = k * jnp.exp(-b)
    kdec = (k * jnp.exp(b_end - b)).astype(BF16)

    lane_k = lax.broadcasted_iota(jnp.int32, (C, C_KEY_WIDTH), 1) // C_K_DIM
    kstack = jnp.concatenate([jnp.where(lane_k == hh, kt, 0.0) for hh in range(C_HEADS)], axis=0).astype(BF16)
    attn = lax.dot_general(qt, kstack, (((1,), (1,)), ((), ())), preferred_element_type=F32)
    ri = lax.broadcasted_iota(jnp.int32, (C, C_HEADS * C), 0)
    cj = lax.broadcasted_iota(jnp.int32, (C, C_HEADS * C), 1) % C
    keep = (cj >= ri) if reverse else (cj <= ri)
    attn = jnp.where(keep, attn, 0.0).astype(BF16)
    lane_v = lax.broadcasted_iota(jnp.int32, (C, C_WIDTH), 1) // C_V_DIM
    vstack = jnp.concatenate([jnp.where(lane_v == hh, v, jnp.zeros_like(v)) for hh in range(C_HEADS)], axis=0)
    o = jnp.dot(attn, vstack, preferred_element_type=F32)
    o = o + lax.dot_general(qt, st.astype(BF16), (((1,), (1,)), ((), ())), preferred_element_type=F32)

    upd = lax.dot_general(v, kdec, (((0,), (0,)), ((), ())), preferred_element_type=F32)
    rh = lax.broadcasted_iota(jnp.int32, (C_WIDTH, C_KEY_WIDTH), 0) // C_V_DIM
    ch = lax.broadcasted_iota(jnp.int32, (C_WIDTH, C_KEY_WIDTH), 1) // C_K_DIM
    st = st * jnp.exp(b_end) + jnp.where(rh == ch, upd, 0.0)
    return o, st


def _gla_fwd_kernel(q_ref, k_ref, v_ref, la_ref, o_ref, st_ref, *, nchunk):
    @pl.when(pl.program_id(1) == 0)
    def _():
        st_ref[...] = jnp.zeros_like(st_ref)

    st = st_ref[...]
    for c in range(nchunk):
        rows = slice(c * C_CHUNK, (c + 1) * C_CHUNK)
        o, st = _gla_chunk(q_ref[0, rows, :], k_ref[0, rows, :], v_ref[0, rows, :], la_ref[0, rows, :], st, False)
        o_ref[0, rows, :] = o
    st_ref[...] = st


def _gla_bwd_kernel(q_ref, k_ref, v_ref, la_ref, of_ref, g_ref, gn_ref, o_ref, st_ref, *, nchunk):
    @pl.when(pl.program_id(1) == 0)
    def _():
        st_ref[...] = jnp.zeros_like(st_ref)

    st = st_ref[...]
    lane_h = lax.broadcasted_iota(jnp.int32, (C_WIDTH, C_WIDTH), 0) // C_V_DIM
    lane_h2 = lax.broadcasted_iota(jnp.int32, (C_WIDTH, C_WIDTH), 1) // C_V_DIM
    seg = (lane_h == lane_h2).astype(F32) * (1.0 / C_V_DIM)
    for c in reversed(range(nchunk)):
        rows = slice(c * C_CHUNK, (c + 1) * C_CHUNK)
        o, st = _gla_chunk(q_ref[0, rows, :], k_ref[0, rows, :], v_ref[0, rows, :], la_ref[0, rows, :], st, True)
        o = o + of_ref[0, rows, :]
        sq = o * o
        sq_hi = sq.astype(BF16)
        sq_lo = (sq - sq_hi.astype(F32)).astype(BF16)
        segb = seg.astype(BF16)
        ms = (jnp.dot(sq_hi, segb, preferred_element_type=F32) + jnp.dot(sq_lo, segb, preferred_element_type=F32))
        o = o * lax.rsqrt(ms + EPS) * gn_ref[...]
        g = g_ref[0, rows, :]
        o_ref[0, rows, :] = (o * (g * jax.nn.sigmoid(g))).astype(o_ref.dtype)
    st_ref[...] = st


def _gla(cq, ck, cv, cg, laf, lab, gn_tiled, rb):
    B, S, _ = cq.shape
    nb = S // rb
    nchunk = rb // C_CHUNK
    fspec = lambda n: pl.BlockSpec((1, rb, n), lambda b, i: (b, i, 0))
    bspec = lambda n: pl.BlockSpec((1, rb, n), lambda b, i: (b, nb - 1 - i, 0))
    st = pltpu.VMEM((C_WIDTH, C_KEY_WIDTH), F32)
    o_f = pl.pallas_call(
        functools.partial(_gla_fwd_kernel, nchunk=nchunk),
        grid=(B, nb),
        in_specs=[fspec(C_KEY_WIDTH), fspec(C_KEY_WIDTH), fspec(C_WIDTH), fspec(C_KEY_WIDTH)],
        out_specs=fspec(C_WIDTH),
        out_shape=jax.ShapeDtypeStruct((B, S, C_WIDTH), F32),
        scratch_shapes=[st],
        compiler_params=_cparams(("parallel", "arbitrary")),
        name="gla_fwd",
    )(cq, ck, cv, laf)
    return pl.pallas_call(
        functools.partial(_gla_bwd_kernel, nchunk=nchunk),
        grid=(B, nb),
        in_specs=[bspec(C_KEY_WIDTH), bspec(C_KEY_WIDTH), bspec(C_WIDTH), bspec(C_KEY_WIDTH),
                  bspec(C_WIDTH), bspec(C_WIDTH), pl.BlockSpec((1, C_WIDTH), lambda b, i: (0, 0))],
        out_specs=bspec(C_WIDTH),
        out_shape=jax.ShapeDtypeStruct((B, S, C_WIDTH), BF16),
        scratch_shapes=[st],
        compiler_params=_cparams(("parallel", "arbitrary")),
        name="gla_bwd",
    )(cq, ck, cv, lab, o_f, cg, gn_tiled)


def _outproj_kernel(x_ref, oa_ref, ob_ref, oc_ref, w_ref, g2_ref, wr_hi_ref, wr_lo_ref,
                    h_ref, hn_ref, aff_ref):
    mix = jnp.concatenate([oa_ref[...], ob_ref[...], oc_ref[...]], axis=1)
    h = x_ref[...] + jnp.dot(mix, w_ref[...], preferred_element_type=F32)
    h_ref[...] = h
    ms = jnp.mean(h * h, axis=-1, keepdims=True)
    hn = h * lax.rsqrt(ms + EPS) * g2_ref[...]
    hn_ref[...] = hn.astype(hn_ref.dtype)
    hn_hi = hn.astype(BF16)
    hn_lo = (hn - hn_hi.astype(F32)).astype(BF16)
    logits = (jnp.dot(hn_hi, wr_hi_ref[...], preferred_element_type=F32)
              + jnp.dot(hn_hi, wr_lo_ref[...], preferred_element_type=F32)
              + jnp.dot(hn_lo, wr_hi_ref[...], preferred_element_type=F32))
    lane = lax.broadcasted_iota(jnp.int32, logits.shape, 1)
    logits = jnp.where(lane < N_EXPERTS, logits, -jnp.inf)
    mx = jnp.max(logits, axis=-1, keepdims=True)
    e = jnp.exp(logits - mx)
    aff = e / jnp.sum(e, axis=-1, keepdims=True)
    aff_ref[...] = jnp.transpose(aff)[:N_EXPERTS, :]


def _outproj(x2d, oa, ob, oc, w, g2, wr_hi, wr_lo, tm):
    T, D = x2d.shape
    row = lambda n: pl.BlockSpec((tm, n), lambda i: (i, 0))
    full = lambda a: pl.BlockSpec(a.shape, lambda i: (0,) * a.ndim)
    return pl.pallas_call(
        _outproj_kernel,
        grid=(T // tm,),
        in_specs=[row(D), row(A_WIDTH), row(B_WIDTH), row(C_WIDTH), full(w), full(g2), full(wr_hi), full(wr_lo)],
        out_specs=[row(D), row(D), pl.BlockSpec((N_EXPERTS, tm), lambda i: (0, i))],
        out_shape=[jax.ShapeDtypeStruct((T, D), F32), jax.ShapeDtypeStruct((T, D), BF16),
                   jax.ShapeDtypeStruct((N_EXPERTS, T), F32)],
        compiler_params=_cparams(("parallel",)),
        name="outproj_router",
    )(x2d, oa, ob, oc, w, g2, wr_hi, wr_lo)


def _ffn_kernel(x_ref, gate_ref, wg_ref, wu_ref, wd_ref, o_ref, *, fc):
    x = x_ref[0]
    F = wg_ref.shape[2]
    acc = jnp.zeros((x.shape[0], o_ref.shape[2]), F32)
    for c in range(F // fc):
        cols = slice(c * fc, (c + 1) * fc)
        g = jnp.dot(x, wg_ref[0, :, cols], preferred_element_type=F32)
        u = jnp.dot(x, wu_ref[0, :, cols], preferred_element_type=F32)
        hid = (g * jax.nn.sigmoid(g) * u).astype(BF16)
        acc = acc + jnp.dot(hid, wd_ref[0, cols, :], preferred_element_type=F32)
    o_ref[0] = (acc * gate_ref[0]).astype(o_ref.dtype)


def _ffn(xe, gate, wg, wu, wd, tm, fc):
    E, cap, D = xe.shape
    F = wg.shape[2]
    fc = min(fc, F)
    return pl.pallas_call(
        functools.partial(_ffn_kernel, fc=fc),
        grid=(E, cap // tm),
        in_specs=[pl.BlockSpec((1, tm, D), lambda e, i: (e, i, 0)),
                  pl.BlockSpec((1, tm, 1), lambda e, i: (e, i, 0)),
                  pl.BlockSpec((1, D, F), lambda e, i: (e, 0, 0)),
                  pl.BlockSpec((1, D, F), lambda e, i: (e, 0, 0)),
                  pl.BlockSpec((1, F, D), lambda e, i: (e, 0, 0))],
        out_specs=pl.BlockSpec((1, tm, D), lambda e, i: (e, i, 0)),
        out_shape=jax.ShapeDtypeStruct((E, cap, D), F32),
        compiler_params=_cparams(("parallel", "arbitrary")),
        name="expert_ffn",
    )(xe, gate, wg, wu, wd)


def _final_norm_kernel(x_ref, g_ref, o_ref):
    x = x_ref[...]
    ms = jnp.mean(x * x, axis=-1, keepdims=True)
    o_ref[...] = x * lax.rsqrt(ms + EPS) * g_ref[...]


def _final_norm(x2d, g, tm):
    T, D = x2d.shape
    return pl.pallas_call(
        _final_norm_kernel,
        grid=(T // tm,),
        in_specs=[pl.BlockSpec((tm, D), lambda i: (i, 0)), pl.BlockSpec((1, D), lambda i: (0, 0))],
        out_specs=pl.BlockSpec((tm, D), lambda i: (i, 0)),
        out_shape=jax.ShapeDtypeStruct((T, D), F32),
        compiler_params=_cparams(("parallel",)),
        name="final_norm",
    )(x2d, g)


def _prep_layer(l, norm1, w_in, ln_v_g, ln_v_b, w_spatial, b_spatial, gla_decay_w, gla_decay_b, gla_norm,
                w_out, norm2, w_router, diff_subln):
    D = w_in.shape[1]
    w = jnp.pad(w_in[l], ((0, 0), (0, PROJ_PAD - PROJ_WIDTH))).astype(BF16)
    bs = jnp.repeat(b_spatial[l].T, B_GROUP_DIM, axis=1)
    wdec = jnp.zeros((128, 2 * C_KEY_WIDTH), F32)
    wdec = wdec.at[:C_DECAY_RANK, :C_KEY_WIDTH].set(gla_decay_w[l, 0])
    wdec = wdec.at[C_DECAY_RANK:2 * C_DECAY_RANK, C_KEY_WIDTH:].set(gla_decay_w[l, 1])
    bdec = gla_decay_b[l].reshape(1, 2 * C_KEY_WIDTH)
    wr = jnp.pad(w_router[l], ((0, 0), (0, 128 - N_EXPERTS)))
    wr_hi = wr.astype(BF16)
    wr_lo = (wr - wr_hi.astype(F32)).astype(BF16)
    return dict(
        g1=norm1[l].reshape(1, D), w=w, lng=ln_v_g[l].reshape(1, B_WIDTH), lnb=ln_v_b[l].reshape(1, B_WIDTH),
        ws=w_spatial[l].astype(BF16), bs=bs, wdec=wdec.astype(BF16), bdec=bdec,
        gn=jnp.tile(gla_norm[l], C_HEADS).reshape(1, C_WIDTH), w_out=w_out[l].astype(BF16),
        g2=norm2[l].reshape(1, D), wr_hi=wr_hi, wr_lo=wr_lo, subln=diff_subln[l].reshape(A_V_DIM, 1),
    )


def _tile(n, pref):
    t = min(n, pref)
    while n % t:
        t //= 2
    return t


def _trunk(x, layers, lam_vecs, ffn_w, norm_f):
    B, S, D = x.shape
    T = B * S
    cap = EC_CAPACITY_FACTOR * T // N_EXPERTS
    tm = _tile(T, 512)
    tk = _tile(S, 512)
    tq = min(tk, 256)
    kpos = _key_pos_lanes(tk)
    x2d = x.reshape(T, D)
    for l, p in enumerate(layers):
        lam_init = 0.8 - 0.6 * math.exp(-0.3 * l)
        aq, ak, avt, ob, cq, ck, cv, cg, laf, lab = _inproj(
            x2d, p["g1"], p["w"], p["lng"], p["lnb"], p["ws"], p["bs"], p["wdec"], p["bdec"], tm, min(tk, tm))
        r3 = lambda a: a.reshape(B, S, a.shape[-1])
        lq1, lk1, lq2, lk2 = (v[l:l + 1] for v in lam_vecs)
        oa = _attention(r3(aq), r3(ak), avt, lq1, lk1, lq2, lk2, p["subln"], kpos, lam_init, tq, tk)
        oc = _gla(r3(cq), r3(ck), r3(cv), r3(cg), r3(laf), r3(lab), p["gn"], _tile(S, 512))
        h, hn, aff_t = _outproj(x2d, oa.reshape(T, A_WIDTH), ob, oc.reshape(T, C_WIDTH),
                                p["w_out"], p["g2"], p["wr_hi"], p["wr_lo"], tm)
        gate, idx = lax.top_k(aff_t, cap)
        xe = hn[idx]
        wg, wu, wd = (w[l] for w in ffn_w)
        ye = _ffn(xe, gate[..., None], wg, wu, wd, _tile(cap, 512), 512)
        x2d = h.at[idx.reshape(-1)].add(ye.reshape(-1, D))
    return _final_norm(x2d, norm_f.reshape(1, D), tm).reshape(B, S, D)


def kernel(x_prompt, x_sample, norm1, w_in, lam_q1, lam_k1, lam_q2, lam_k2, diff_subln, ln_v_g, ln_v_b,
           w_spatial, b_spatial, gla_decay_w, gla_decay_b, gla_norm, w_out, norm2, w_router, w_gate, w_up,
           w_down, norm_f):
    depth = w_in.shape[0]
    layers = [_prep_layer(l, norm1, w_in, ln_v_g, ln_v_b, w_spatial, b_spatial, gla_decay_w, gla_decay_b,
                          gla_norm, w_out, norm2, w_router, diff_subln) for l in range(depth)]
    ffn_w = (w_gate.astype(BF16), w_up.astype(BF16), w_down.astype(BF16))
    lam_vecs = (lam_q1, lam_k1, lam_q2, lam_k2)
    y_prompt = _trunk(x_prompt, layers, lam_vecs, ffn_w, norm_f)
    y_sample = _trunk(x_sample, layers, lam_vecs, ffn_w, norm_f)
    return (y_prompt, y_sample)
```

```python
import functools
import math

import jax
import jax.numpy as jnp
from jax import lax
from jax.experimental import pallas as pl
from jax.experimental.pallas import tpu as pltpu
from jax.experimental.pallas import tpu_sc as plsc

F32 = jnp.float32
BF16 = jnp.bfloat16

EPS = 1e-6
LOG2E = 1.4426950408889634

A_HEADS = 4
A_QK_DIM = 64
A_V_DIM = 128
A_WIDTH = A_HEADS * A_V_DIM
B_GROUPS = 4
B_WIDTH = 256
B_GROUP_DIM = 64
B_CHUNK = 128
C_HEADS = 4
C_WIDTH = 256
C_V_DIM = 64
C_K_DIM = 32
C_KEY_WIDTH = 128
C_DECAY_RANK = 16
C_GATE_NORMALIZER = 16.0
C_CHUNK = 64
N_EXPERTS = 16
EC_CAPACITY_FACTOR = 2

OFF_AQ, OFF_AK, OFF_AV = 0, 512, 1024
OFF_BU, OFF_BV = 1536, 1792
OFF_CQ, OFF_CK, OFF_CV, OFF_CG, OFF_CZ = 2048, 2176, 2304, 2560, 2816
PROJ_WIDTH = 2848
PROJ_PAD = 2944

VMEM_LIMIT = 56 * 1024 * 1024


def _cparams(sem):
    return pltpu.CompilerParams(dimension_semantics=sem, vmem_limit_bytes=VMEM_LIMIT)


def _gelu_tanh(x):
    return 0.5 * x * (1.0 + jnp.tanh(0.7978845608028654 * (x + 0.044715 * x * x * x)))


def _log_sigmoid(x):
    return jnp.minimum(x, 0.0) - jnp.log(1.0 + jnp.exp(-jnp.abs(x)))


def _inproj_kernel(x_ref, g1_ref, w_ref, lng_ref, lnb_ref, ws_ref, bs_ref, wdec_ref, bdec_ref,
                   aq_ref, ak_ref, av_ref, ob_ref, cq_ref, ck_ref, cv_ref, cg_ref, laf_ref, lab_ref):
    x = x_ref[...]
    ms = jnp.mean(x * x, axis=-1, keepdims=True)
    hn = (x * lax.rsqrt(ms + EPS) * g1_ref[...]).astype(BF16)
    proj = jnp.dot(hn, w_ref[...], preferred_element_type=F32)

    aq_ref[...] = (proj[:, OFF_AQ:OFF_AK] * (A_QK_DIM ** -0.5 * LOG2E)).astype(BF16)
    ak_ref[...] = proj[:, OFF_AK:OFF_AV].astype(BF16)
    tkb = av_ref.shape[2]
    for c in range(av_ref.shape[0]):
        av_ref[c] = jnp.transpose(proj[c * tkb:(c + 1) * tkb, OFF_AV:OFF_BU]).astype(BF16)

    u = _gelu_tanh(proj[:, OFF_BU:OFF_BV])
    v = _gelu_tanh(proj[:, OFF_BV:OFF_CQ])
    mu = jnp.mean(v, axis=-1, keepdims=True)
    var = jnp.mean(jnp.square(v - mu), axis=-1, keepdims=True)
    v = ((v - mu) * lax.rsqrt(var + EPS) * lng_ref[...] + lnb_ref[...]).astype(BF16)
    tm = x.shape[0]
    lane = lax.broadcasted_iota(jnp.int32, (B_CHUNK, 128), 1)
    first_half = lane < B_GROUP_DIM
    for c in range(tm // B_CHUNK):
        rows = slice(c * B_CHUNK, (c + 1) * B_CHUNK)
        parts = []
        for p in range(B_GROUPS // 2):
            vch = v[rows, p * 128:(p + 1) * 128]
            m0 = jnp.dot(ws_ref[2 * p], vch, preferred_element_type=F32)
            m1 = jnp.dot(ws_ref[2 * p + 1], vch, preferred_element_type=F32)
            parts.append(jnp.where(first_half, m0, m1))
        mixed = jnp.concatenate(parts, axis=1) + bs_ref[...]
        ob_ref[rows, :] = (u[rows, :] * mixed).astype(BF16)

    cq_ref[...] = proj[:, OFF_CQ:OFF_CK] * (C_K_DIM ** -0.5)
    ck_ref[...] = proj[:, OFF_CK:OFF_CV]
    cv_ref[...] = proj[:, OFF_CV:OFF_CG].astype(BF16)
    cg_ref[...] = proj[:, OFF_CG:OFF_CZ]
    z = proj[:, OFF_CZ:PROJ_PAD].astype(BF16)
    xd = jnp.dot(z, wdec_ref[...], preferred_element_type=F32) + bdec_ref[...]
    la = _log_sigmoid(xd) * (1.0 / C_GATE_NORMALIZER)
    laf_ref[...] = la[:, :C_KEY_WIDTH]
    lab_ref[...] = la[:, C_KEY_WIDTH:]


def _inproj(x2d, g1, w, lng, lnb, ws, bs, wdec, bdec, tm, tk):
    T, D = x2d.shape
    row = lambda n: pl.BlockSpec((tm, n), lambda i: (i, 0))
    full = lambda a: pl.BlockSpec(a.shape, lambda i: (0,) * a.ndim)
    outs = [
        (A_WIDTH, BF16), (A_WIDTH, BF16), None, (B_WIDTH, BF16),
        (C_KEY_WIDTH, F32), (C_KEY_WIDTH, F32), (C_WIDTH, BF16), (C_WIDTH, F32),
        (C_KEY_WIDTH, F32), (C_KEY_WIDTH, F32),
    ]
    vt_spec = pl.BlockSpec((tm // tk, A_WIDTH, tk), lambda i: (i, 0, 0))
    vt_shape = jax.ShapeDtypeStruct((T // tk, A_WIDTH, tk), BF16)
    return pl.pallas_call(
        _inproj_kernel,
        grid=(T // tm,),
        in_specs=[row(D), full(g1), full(w), full(lng), full(lnb), full(ws), full(bs), full(wdec), full(bdec)],
        out_specs=[vt_spec if o is None else row(o[0]) for o in outs],
        out_shape=[vt_shape if o is None else jax.ShapeDtypeStruct((T, o[0]), o[1]) for o in outs],
        compiler_params=_cparams(("parallel",)),
        name="inproj",
    )(x2d, g1, w, lng, lnb, ws, bs, wdec, bdec)


N_POS_LANES = 9


def _split3(x):
    hi = x.astype(BF16)
    r1 = x - hi.astype(F32)
    mid = r1.astype(BF16)
    lo = (r1 - mid.astype(F32)).astype(BF16)
    return hi, mid, lo


def _attn_kernel(lq1_ref, lk1_ref, lq2_ref, lk2_ref, g_ref, kpos_ref, q_ref, k_ref, vt_ref, o_ref,
                 qall_ref, *chain_refs, tq, tk, nq, lam_init):
    nch = 2 * nq
    grp = lambda g: chain_refs[g * nch:(g + 1) * nch]
    m_refs, l_refs, acc_refs = grp(0), grp(1), grp(2)
    s_refs, p_refs, al_refs = (grp(3), grp(4)), (grp(5), grp(6)), (grp(7), grp(8))
    h = pl.program_id(1)
    qi = pl.program_id(2)
    nblk = k_ref.shape[1] // tk
    nrest = nblk - 1
    lam = (jnp.exp(jnp.sum(lq1_ref[...] * lk1_ref[...], axis=-1, keepdims=True))
           - jnp.exp(jnp.sum(lq2_ref[...] * lk2_ref[...], axis=-1, keepdims=True)) + lam_init)
    c = jnp.exp2(-8.0 * (h + 1).astype(F32) / A_HEADS) * LOG2E

    lane = lax.broadcasted_iota(jnp.int32, (tq, 128), 1)
    il = lax.broadcasted_iota(jnp.int32, (tq, 128), 0).astype(F32)
    hi, mid, lo = _split3(jnp.where(lane < 3, -c * il, c))
    piece = lane % 3
    qpos = jnp.where(piece == 0, hi, jnp.where(piece == 1, mid, lo))
    qpos = jnp.where(lane < N_POS_LANES, qpos, jnp.zeros_like(qpos))
    chains = [(t, m) for t in range(nq) for m in range(2)]
    for ci, (t, m) in enumerate(chains):
        q = q_ref[0, t * tq:(t + 1) * tq, :]
        keep = (lane < A_QK_DIM) if m == 0 else (lane >= A_QK_DIM)
        qm = jnp.where(keep, q, jnp.zeros_like(q))
        qall_ref[0, ci] = jnp.concatenate([qm, qpos], axis=1)
        qall_ref[1, ci] = jnp.concatenate([qm, -qpos], axis=1)
    kpos = kpos_ref[...]

    def block_of(i):
        above = (i >= qi).astype(jnp.int32)
        return i + above, above

    def scores(j, variant, ci):
        kaug = jnp.concatenate([k_ref[0, pl.ds(pl.multiple_of(j * tk, tk), tk), :], kpos], axis=1)
        return lax.dot_general(kaug, qall_ref[variant, ci], (((1,), (1,)), ((), ())),
                               preferred_element_type=F32)

    def softmax(j, ci, s, first):
        t, _ = chains[ci]
        off = ((qi * nq + t) * tq - j * tk).astype(F32)
        delta = c * jnp.abs(off)
        bmax = jnp.max(s, axis=0, keepdims=True) - delta
        if first:
            mn = bmax
        else:
            mx = m_refs[ci][...]
            mn = jnp.maximum(mx, bmax)
        p = jnp.exp2(s - (mn + delta))
        psum = jnp.sum(p, axis=0, keepdims=True)
        m_refs[ci][...] = mn
        if first:
            l_refs[ci][...] = psum
            return p.astype(BF16), None
        alpha = jnp.exp2(mx - mn)
        l_refs[ci][...] = alpha * l_refs[ci][...] + psum
        return p.astype(BF16), alpha

    for ci, (t, m) in enumerate(chains):
        s = scores(qi, 0, ci)
        rel = (lax.broadcasted_iota(jnp.int32, (tk, tq), 0)
               - lax.broadcasted_iota(jnp.int32, (tk, tq), 1)).astype(F32)
        s = s - (2.0 * c) * jnp.maximum(rel - float(t * tq), 0.0)
        p, _ = softmax(qi, ci, s, True)
        acc_refs[ci][...] = jnp.dot(vt_ref[qi], p, preferred_element_type=F32)

    def tick(tau, par, do_qk, do_sm, do_pv):
        if do_qk:
            jq, vq = block_of(tau)
        if do_sm:
            js, _ = block_of(tau - 1)
        if do_pv:
            jp, _ = block_of(tau - 2)
        for half in range(0, nch, 2):
            for ci in (half, half + 1):
                if do_qk:
                    s_refs[par][ci][...] = scores(jq, vq, ci)
            for ci in (half, half + 1):
                if do_sm:
                    p, alpha = softmax(js, ci, s_refs[1 - par][ci][...], False)
                    p_refs[1 - par][ci][...] = p
                    al_refs[1 - par][ci][...] = alpha
            for ci in (half, half + 1):
                if do_pv:
                    pv = jnp.dot(vt_ref[jp], p_refs[par][ci][...], preferred_element_type=F32)
                    acc_refs[ci][...] = al_refs[par][ci][...] * acc_refs[ci][...] + pv

    L = nrest
    if L > 0:
        assert L % 2 == 1
        tick(0, 0, True, False, False)
        tick(1, 1, L > 1, True, False)
        if L > 1:
            tick(2, 0, True, True, True)

            def pair(ip, carry):
                tau = 3 + 2 * ip
                tick(tau, 1, True, True, True)
                tick(tau + 1, 0, True, True, True)
                return carry

            lax.fori_loop(0, (L - 3) // 2, pair, 0)
            tick(L, 1, False, True, True)
        tick(L + 1, (L + 1) % 2, False, False, True)

    for t in range(nq):
        a0, a1 = acc_refs[2 * t][...], acc_refs[2 * t + 1][...]
        o = a0 / l_refs[2 * t][...] - lam * (a1 / l_refs[2 * t + 1][...])
        ms = jnp.mean(o * o, axis=0, keepdims=True)
        o = o * lax.rsqrt(ms + EPS) * g_ref[...] * (1.0 - lam_init)
        o_ref[0, t * tq:(t + 1) * tq, :] = jnp.transpose(o).astype(o_ref.dtype)


def _attention(aq, ak, avt, lq1, lk1, lq2, lk2, subln_col, kpos, lam_init, tq, tk):
    B, S, _ = aq.shape
    nq = tk // tq
    nch = 2 * nq
    assert S == tk or (S // tk) % 2 == 0
    vec = pl.BlockSpec((1, A_QK_DIM), lambda b, h, i: (0, 0))
    kern = functools.partial(_attn_kernel, tq=tq, tk=tk, nq=nq, lam_init=lam_init)
    return pl.pallas_call(
        kern,
        grid=(B, A_HEADS, S // tk),
        in_specs=[vec, vec, vec, vec,
                  pl.BlockSpec((A_V_DIM, 1), lambda b, h, i: (0, 0)),
                  pl.BlockSpec((tk, 128), lambda b, h, i: (0, 0)),
                  pl.BlockSpec((1, tk, 128), lambda b, h, i: (b, i, h)),
                  pl.BlockSpec((1, S, 128), lambda b, h, i: (b, 0, h)),
                  pl.BlockSpec((S // tk, A_V_DIM, tk), lambda b, h, i: (b, h, 0))],
        out_specs=pl.BlockSpec((1, tk, 128), lambda b, h, i: (b, i, h)),
        out_shape=jax.ShapeDtypeStruct((B, S, A_WIDTH), BF16),
        scratch_shapes=([pltpu.VMEM((2, nch, tq, 256), BF16)]
                        + [pltpu.VMEM((1, tq), F32)] * (2 * nch)
                        + [pltpu.VMEM((A_V_DIM, tq), F32)] * nch
                        + [pltpu.VMEM((tk, tq), F32)] * (2 * nch)
                        + [pltpu.VMEM((tk, tq), BF16)] * (2 * nch)
                        + [pltpu.VMEM((1, tq), F32)] * (2 * nch)),
        compiler_params=_cparams(("parallel", "parallel", "arbitrary")),
        name="diff_attn",
    )(lq1, lk1, lq2, lk2, subln_col, kpos, aq, ak, avt)


def _key_pos_lanes(tk):
    j = jnp.arange(tk, dtype=jnp.int32)
    jlo = (j % 256).astype(F32)
    jhi = (j - j % 256).astype(F32)
    cols = [jnp.ones((tk,), F32)] * 3 + [jlo] * 3 + [jhi] * 3
    kp = jnp.stack(cols, axis=1)
    return jnp.pad(kp, ((0, 0), (0, 128 - N_POS_LANES))).astype(BF16)


def _gla_chunk(q, k, v, la, st, reverse):
    C = C_CHUNK
    r = lax.broadcasted_iota(jnp.int32, (C, C), 0)
    c = lax.broadcasted_iota(jnp.int32, (C, C), 1)
    tri = (c >= r) if reverse else (c <= r)
    tri_b = tri.astype(BF16)
    la_hi = la.astype(BF16)
    la_lo = (la - la_hi.astype(F32)).astype(BF16)
    b = (jnp.dot(tri_b, la_hi, preferred_element_type=F32)
         + jnp.dot(tri_b, la_lo, preferred_element_type=F32))
    b_end = b[0:1, :] if reverse else b[C - 1:C, :]
    eb = jnp.exp(b)
    qt = (q * eb).astype(BF16)
    kt = k * jnp.exp(-b)
    kdec = (k * jnp.exp(b_end - b)).astype(BF16)

    lane_k = lax.broadcasted_iota(jnp.int32, (C, C_KEY_WIDTH), 1) // C_K_DIM
    kstack = jnp.concatenate([jnp.where(lane_k == hh, kt, 0.0) for hh in range(C_HEADS)], axis=0).astype(BF16)
    attn = lax.dot_general(qt, kstack, (((1,), (1,)), ((), ())), preferred_element_type=F32)
    ri = lax.broadcasted_iota(jnp.int32, (C, C_HEADS * C), 0)
    cj = lax.broadcasted_iota(jnp.int32, (C, C_HEADS * C), 1) % C
    keep = (cj >= ri) if reverse else (cj <= ri)
    attn = jnp.where(keep, attn, 0.0).astype(BF16)
    lane_v = lax.broadcasted_iota(jnp.int32, (C, C_WIDTH), 1) // C_V_DIM
    vstack = jnp.concatenate([jnp.where(lane_v == hh, v, jnp.zeros_like(v)) for hh in range(C_HEADS)], axis=0)
    o = jnp.dot(attn, vstack, preferred_element_type=F32)
    o = o + lax.dot_general(qt, st.astype(BF16), (((1,), (1,)), ((), ())), preferred_element_type=F32)

    upd = lax.dot_general(v, kdec, (((0,), (0,)), ((), ())), preferred_element_type=F32)
    rh = lax.broadcasted_iota(jnp.int32, (C_WIDTH, C_KEY_WIDTH), 0) // C_V_DIM
    ch = lax.broadcasted_iota(jnp.int32, (C_WIDTH, C_KEY_WIDTH), 1) // C_K_DIM
    st = st * jnp.exp(b_end) + jnp.where(rh == ch, upd, 0.0)
    return o, st


def _gla_fwd_kernel(q_ref, k_ref, v_ref, la_ref, o_ref, st_ref, *, nchunk):
    @pl.when(pl.program_id(1) == 0)
    def _():
        st_ref[...] = jnp.zeros_like(st_ref)

    st = st_ref[...]
    for c in range(nchunk):
        rows = slice(c * C_CHUNK, (c + 1) * C_CHUNK)
        o, st = _gla_chunk(q_ref[0, rows, :], k_ref[0, rows, :], v_ref[0, rows, :], la_ref[0, rows, :], st, False)
        o_ref[0, rows, :] = o
    st_ref[...] = st


def _gla_bwd_kernel(q_ref, k_ref, v_ref, la_ref, of_ref, g_ref, gn_ref, o_ref, st_ref, *, nchunk):
    @pl.when(pl.program_id(1) == 0)
    def _():
        st_ref[...] = jnp.zeros_like(st_ref)

    st = st_ref[...]
    lane_h = lax.broadcasted_iota(jnp.int32, (C_WIDTH, C_WIDTH), 0) // C_V_DIM
    lane_h2 = lax.broadcasted_iota(jnp.int32, (C_WIDTH, C_WIDTH), 1) // C_V_DIM
    seg = (lane_h == lane_h2).astype(F32) * (1.0 / C_V_DIM)
    for c in reversed(range(nchunk)):
        rows = slice(c * C_CHUNK, (c + 1) * C_CHUNK)
        o, st = _gla_chunk(q_ref[0, rows, :], k_ref[0, rows, :], v_ref[0, rows, :], la_ref[0, rows, :], st, True)
        o = o + of_ref[0, rows, :]
        sq = o * o
        sq_hi = sq.astype(BF16)
        sq_lo = (sq - sq_hi.astype(F32)).astype(BF16)
        segb = seg.astype(BF16)
        ms = (jnp.dot(sq_hi, segb, preferred_element_type=F32) + jnp.dot(sq_lo, segb, preferred_element_type=F32))
        o = o * lax.rsqrt(ms + EPS) * gn_ref[...]
        g = g_ref[0, rows, :]
        o_ref[0, rows, :] = (o * (g * jax.nn.sigmoid(g))).astype(o_ref.dtype)
    st_ref[...] = st


def _gla(cq, ck, cv, cg, laf, lab, gn_tiled, rb):
    B, S, _ = cq.shape
    nb = S // rb
    nchunk = rb // C_CHUNK
    fspec = lambda n: pl.BlockSpec((1, rb, n), lambda b, i: (b, i, 0))
    bspec = lambda n: pl.BlockSpec((1, rb, n), lambda b, i: (b, nb - 1 - i, 0))
    st = pltpu.VMEM((C_WIDTH, C_KEY_WIDTH), F32)
    o_f = pl.pallas_call(
        functools.partial(_gla_fwd_kernel, nchunk=nchunk),
        grid=(B, nb),
        in_specs=[fspec(C_KEY_WIDTH), fspec(C_KEY_WIDTH), fspec(C_WIDTH), fspec(C_KEY_WIDTH)],
        out_specs=fspec(C_WIDTH),
        out_shape=jax.ShapeDtypeStruct((B, S, C_WIDTH), F32),
        scratch_shapes=[st],
        compiler_params=_cparams(("parallel", "arbitrary")),
        name="gla_fwd",
    )(cq, ck, cv, laf)
    return pl.pallas_call(
        functools.partial(_gla_bwd_kernel, nchunk=nchunk),
        grid=(B, nb),
        in_specs=[bspec(C_KEY_WIDTH), bspec(C_KEY_WIDTH), bspec(C_WIDTH), bspec(C_KEY_WIDTH),
                  bspec(C_WIDTH), bspec(C_WIDTH), pl.BlockSpec((1, C_WIDTH), lambda b, i: (0, 0))],
        out_specs=bspec(C_WIDTH),
        out_shape=jax.ShapeDtypeStruct((B, S, C_WIDTH), BF16),
        scratch_shapes=[st],
        compiler_params=_cparams(("parallel", "arbitrary")),
        name="gla_bwd",
    )(cq, ck, cv, lab, o_f, cg, gn_tiled)


def _outproj_kernel(x_ref, oa_ref, ob_ref, oc_ref, w_ref, g2_ref, wr_hi_ref, wr_lo_ref,
                    h_ref, hn_ref, aff_ref):
    mix = jnp.concatenate([oa_ref[...], ob_ref[...], oc_ref[...]], axis=1)
    h = x_ref[...] + jnp.dot(mix, w_ref[...], preferred_element_type=F32)
    h_ref[...] = h
    ms = jnp.mean(h * h, axis=-1, keepdims=True)
    hn = h * lax.rsqrt(ms + EPS) * g2_ref[...]
    hn_ref[...] = _pack_bf16_pairs(hn)
    hn_hi = hn.astype(BF16)
    hn_lo = (hn - hn_hi.astype(F32)).astype(BF16)
    logits = (jnp.dot(hn_hi, wr_hi_ref[...], preferred_element_type=F32)
              + jnp.dot(hn_hi, wr_lo_ref[...], preferred_element_type=F32)
              + jnp.dot(hn_lo, wr_hi_ref[...], preferred_element_type=F32))
    lane = lax.broadcasted_iota(jnp.int32, logits.shape, 1)
    logits = jnp.where(lane < N_EXPERTS, logits, -jnp.inf)
    mx = jnp.max(logits, axis=-1, keepdims=True)
    e = jnp.exp(logits - mx)
    aff = e / jnp.sum(e, axis=-1, keepdims=True)
    aff_ref[...] = jnp.transpose(aff)[:N_EXPERTS, :]


def _outproj(x2d, oa, ob, oc, w, g2, wr_hi, wr_lo, tm):
    T, D = x2d.shape
    row = lambda n: pl.BlockSpec((tm, n), lambda i: (i, 0))
    full = lambda a: pl.BlockSpec(a.shape, lambda i: (0,) * a.ndim)
    return pl.pallas_call(
        _outproj_kernel,
        grid=(T // tm,),
        in_specs=[row(D), row(A_WIDTH), row(B_WIDTH), row(C_WIDTH), full(w), full(g2), full(wr_hi), full(wr_lo)],
        out_specs=[row(D), row(D // 2), pl.BlockSpec((N_EXPERTS, tm), lambda i: (0, i))],
        out_shape=[jax.ShapeDtypeStruct((T, D), F32), jax.ShapeDtypeStruct((T, D // 2), jnp.uint32),
                   jax.ShapeDtypeStruct((N_EXPERTS, T), F32)],
        compiler_params=_cparams(("parallel",)),
        name="outproj_router",
    )(x2d, oa, ob, oc, w, g2, wr_hi, wr_lo)


def _cumsum_tokens(x, upper, lstrict, ones):
    G, R, _ = x.shape
    xb = x.reshape(G * R, 128).astype(BF16)
    within = jnp.dot(xb, upper, preferred_element_type=F32).reshape(G, R, 128)
    rowtot = jnp.dot(xb, ones, preferred_element_type=F32).reshape(G, R, 128)
    hi = jnp.floor(rowtot * (1.0 / 256.0))
    lo = rowtot - 256.0 * hi
    outs = []
    for g in range(G):
        before = (256.0 * jnp.dot(lstrict, hi[g].astype(BF16), preferred_element_type=F32)
                  + jnp.dot(lstrict, lo[g].astype(BF16), preferred_element_type=F32))
        outs.append(within[g] + before)
    return jnp.stack(outs, axis=0)


def _route_kernel(aff_ref, sel_ref, pos_ref, dst_ref, off_ref, cnt_ref, *, cap):
    E, R, _ = aff_ref.shape
    bits = pltpu.bitcast(aff_ref[...].reshape(E * R, 128), jnp.int32).reshape(E, R, 128)

    def count_ge(v):
        return jnp.sum(jnp.sum((bits >= v).astype(jnp.int32), axis=1, keepdims=True), axis=2, keepdims=True)

    def bisect(i, prefix):
        cand = prefix | jnp.left_shift(jnp.int32(1), 30 - i)
        return jnp.where(count_ge(cand) >= cap, cand, prefix)

    thr = lax.fori_loop(0, 31, bisect, jnp.zeros((E, 1, 1), jnp.int32))
    gt = bits > thr
    eq = bits == thr
    n_gt = jnp.sum(jnp.sum(gt.astype(jnp.int32), axis=1, keepdims=True), axis=2, keepdims=True)
    need_eq = (cap - n_gt).astype(F32)

    r = lax.broadcasted_iota(jnp.int32, (128, 128), 0)
    cc = lax.broadcasted_iota(jnp.int32, (128, 128), 1)
    upper = (r <= cc).astype(BF16)
    ones = jnp.ones((128, 128), BF16)
    rr = lax.broadcasted_iota(jnp.int32, (R, R), 0)
    rc = lax.broadcasted_iota(jnp.int32, (R, R), 1)
    lstrict = (rc < rr).astype(BF16)

    eqf = eq.astype(F32)
    eq_rank = _cumsum_tokens(eqf, upper, lstrict, ones) - eqf
    sel = jnp.logical_or(gt, jnp.logical_and(eq, eq_rank < need_eq))
    self_ = sel.astype(F32)
    pos = _cumsum_tokens(self_, upper, lstrict, ones) - self_
    cnt = jnp.sum(self_, axis=0, keepdims=True)
    off = _cumsum_tokens(cnt, upper, lstrict, ones) - cnt
    rank = jnp.zeros((R, 128), F32)
    for e in range(E):
        dst_ref[e] = (off[0] + rank).astype(jnp.int32)
        rank = rank + self_[e]
    sel_ref[...] = sel.astype(jnp.int32)
    pos_ref[...] = pos.astype(jnp.int32)
    off_ref[...] = off[0].astype(jnp.int32)
    cnt_ref[...] = cnt[0].astype(jnp.int32)


def _route(aff_t, cap):
    E, T = aff_t.shape
    R = T // 128
    a3 = aff_t.reshape(E, R, 128)
    i3 = jax.ShapeDtypeStruct((E, R, 128), jnp.int32)
    i2 = jax.ShapeDtypeStruct((R, 128), jnp.int32)
    return pl.pallas_call(
        functools.partial(_route_kernel, cap=cap),
        out_shape=[i3, i3, i3, i2, i2],
        compiler_params=pltpu.CompilerParams(vmem_limit_bytes=VMEM_LIMIT),
        name="route_select",
    )(a3)


SC_CHUNK = 2048
SC_ROWS = 64


def _sc_mesh():
    return plsc.VectorSubcoreMesh(core_axis_name="c", subcore_axis_name="s")


def _sc_worker():
    info = pltpu.get_tpu_info().sparse_core
    return lax.axis_index("s") * info.num_cores + lax.axis_index("c"), info.num_cores * info.num_subcores


def _sc_compact(sel, pos, dst, aff, E, T, cap):
    @functools.partial(
        pl.kernel, mesh=_sc_mesh(),
        out_type=[jax.ShapeDtypeStruct((E * cap,), jnp.int32), jax.ShapeDtypeStruct((E * cap,), F32),
                  jax.ShapeDtypeStruct((E * cap,), jnp.int32)],
        scratch_types=[pltpu.VMEM((SC_CHUNK,), jnp.int32), pltpu.VMEM((SC_CHUNK,), jnp.int32),
                       pltpu.VMEM((SC_CHUNK,), jnp.int32), pltpu.VMEM((SC_CHUNK,), F32),
                       pltpu.VMEM((cap,), jnp.int32), pltpu.VMEM((cap,), F32), pltpu.VMEM((cap,), jnp.int32)],
        compiler_params=pltpu.CompilerParams(needs_layout_passes=False),
    )
    def k(sel_hbm, pos_hbm, dst_hbm, aff_hbm, idx_hbm, gate_hbm, dstl_hbm,
          sel_v, pos_v, dst_v, aff_v, idx_b, gate_b, dstl_b):
        e, _ = _sc_worker()

        @pl.when(e < E)
        def _():
            @pl.loop(0, T // SC_CHUNK)
            def _(ch):
                base = pl.multiple_of(e * T + ch * SC_CHUNK, 8)
                pltpu.sync_copy(sel_hbm.at[pl.ds(base, SC_CHUNK)], sel_v)
                pltpu.sync_copy(pos_hbm.at[pl.ds(base, SC_CHUNK)], pos_v)
                pltpu.sync_copy(dst_hbm.at[pl.ds(base, SC_CHUNK)], dst_v)
                pltpu.sync_copy(aff_hbm.at[pl.ds(base, SC_CHUNK)], aff_v)

                @pl.loop(0, SC_CHUNK // 16)
                def _(i):
                    sl = pl.ds(pl.multiple_of(i * 16, 16), 16)
                    chosen = sel_v[sl] > 0
                    slot = pos_v[sl]
                    tok = ch * SC_CHUNK + i * 16 + lax.iota(jnp.int32, 16)
                    plsc.store_scatter(idx_b, [slot], tok, mask=chosen)
                    plsc.store_scatter(gate_b, [slot], aff_v[sl], mask=chosen)
                    plsc.store_scatter(dstl_b, [slot], dst_v[sl], mask=chosen)

            out = pl.ds(pl.multiple_of(e * cap, 8), cap)
            pltpu.sync_copy(idx_b, idx_hbm.at[out])
            pltpu.sync_copy(gate_b, gate_hbm.at[out])
            pltpu.sync_copy(dstl_b, dstl_hbm.at[out])

    return k(sel, pos, dst, aff)


def _sc_gather_rows(table, idx):
    B, (_, D) = idx.shape[0], table.shape

    @functools.partial(
        pl.kernel, mesh=_sc_mesh(), out_type=jax.ShapeDtypeStruct((B, D), table.dtype),
        scratch_types=[pltpu.VMEM((SC_ROWS,), jnp.int32), pltpu.VMEM((SC_ROWS, D), table.dtype),
                       pltpu.SemaphoreType.DMA])
    def k(table_hbm, idx_hbm, out_hbm, idx_v, rows_v, sem):
        wid, nw = _sc_worker()
        per = B // nw

        @pl.loop(0, per // SC_ROWS)
        def _(g):
            off = pl.multiple_of(wid * per + g * SC_ROWS, 8)
            pltpu.sync_copy(idx_hbm.at[pl.ds(off, SC_ROWS)], idx_v)
            pltpu.async_copy(table_hbm.at[idx_v], rows_v, sem).wait()
            pltpu.sync_copy(rows_v, out_hbm.at[pl.ds(off, SC_ROWS)])

    return k(table, idx)


def _sc_scatter_rows(rows, dst):
    B, D = rows.shape

    @functools.partial(
        pl.kernel, mesh=_sc_mesh(), out_type=jax.ShapeDtypeStruct((B, D), rows.dtype),
        scratch_types=[pltpu.VMEM((SC_ROWS,), jnp.int32), pltpu.VMEM((SC_ROWS, D), rows.dtype),
                       pltpu.SemaphoreType.DMA])
    def k(rows_hbm, dst_hbm, out_hbm, dst_v, rows_v, sem):
        wid, nw = _sc_worker()
        per = B // nw

        @pl.loop(0, per // SC_ROWS)
        def _(g):
            off = pl.multiple_of(wid * per + g * SC_ROWS, 8)
            pltpu.sync_copy(dst_hbm.at[pl.ds(off, SC_ROWS)], dst_v)
            pltpu.sync_copy(rows_hbm.at[pl.ds(off, SC_ROWS)], rows_v)
            pltpu.async_copy(rows_v, out_hbm.at[dst_v], sem).wait()

    return k(rows, dst)


def _pack_bf16_pairs(x):
    m = x.shape[1] // 2
    bits = pltpu.bitcast(x.astype(BF16).astype(F32), jnp.uint32)
    return (bits[:, m:] & jnp.uint32(0xFFFF0000)) | (bits[:, :m] >> 16)


def _unpack_bf16_pairs(u):
    lo = pltpu.bitcast(u << 16, F32)
    hi = pltpu.bitcast(u & jnp.uint32(0xFFFF0000), F32)
    return jnp.concatenate([lo, hi], axis=1).astype(BF16)


def _combine_kernel(start_ref, h_ref, off_ref, cnt_ref, y_hbm, o_ref, ybuf, sem, acc_ref, nwin_done,
                    *, tb, win, nrows):
    b = pl.program_id(0)
    nb = pl.num_programs(0)

    def window_start(blk, w):
        a8 = (start_ref[blk] // 8) * 8
        return pl.multiple_of(jnp.minimum(a8 + w * win, nrows - win), 8)

    def n_windows(blk):
        a8 = (start_ref[blk] // 8) * 8
        return jnp.maximum((start_ref[blk + 1] - a8 + win - 1) // win, 1)

    def copy(blk, w, slot):
        return pltpu.make_async_copy(y_hbm.at[pl.ds(window_start(blk, w), win)], ybuf.at[slot], sem.at[slot])

    @pl.when(b == 0)
    def _():
        nwin_done[0] = 0
        copy(0, 0, 0).start()

    rows = off_ref.shape[0]
    off_row = jnp.concatenate([off_ref[i:i + 1, :] for i in range(rows)], axis=1)
    end_row = off_row + jnp.concatenate([cnt_ref[i:i + 1, :] for i in range(rows)], axis=1)
    acc_ref[...] = jnp.zeros_like(acc_ref)
    nwin = n_windows(b)

    def body(w, carry):
        g = nwin_done[0]
        slot = g % 2
        copy(b, w, slot).wait()

        @pl.when(w + 1 < nwin)
        def _():
            copy(b, w + 1, 1 - slot).start()

        @pl.when(jnp.logical_and(w + 1 == nwin, b + 1 < nb))
        def _():
            copy(b + 1, 0, 1 - slot).start()

        first_new = (start_ref[b] // 8) * 8 + w * win
        r = window_start(b, w) + lax.broadcasted_iota(jnp.int32, (win, tb), 0)
        onehot_t = jnp.logical_and(jnp.logical_and(r >= off_row, r < end_row), r >= first_new)
        onehot_t = jnp.where(onehot_t, 1.0, 0.0).astype(BF16)
        y = _unpack_bf16_pairs(ybuf[slot])
        acc_ref[...] += lax.dot_general(onehot_t, y, (((0,), (0,)), ((), ())), preferred_element_type=F32)
        nwin_done[0] = g + 1
        return carry

    lax.fori_loop(0, nwin, body, 0)
    o_ref[...] = h_ref[...] + acc_ref[...]


def _combine(h, off2d, cnt2d, blk_start, y_sorted, tb, win):
    T, D = h.shape
    nrows = y_sorted.shape[0]
    win = min(win, nrows)
    kern = functools.partial(_combine_kernel, tb=tb, win=win, nrows=nrows)
    grid_spec = pltpu.PrefetchScalarGridSpec(
        num_scalar_prefetch=1,
        grid=(T // tb,),
        in_specs=[pl.BlockSpec((tb, D), lambda i, s: (i, 0)),
                  pl.BlockSpec((tb // 128, 128), lambda i, s: (i, 0)),
                  pl.BlockSpec((tb // 128, 128), lambda i, s: (i, 0)),
                  pl.BlockSpec(memory_space=pl.ANY)],
        out_specs=pl.BlockSpec((tb, D), lambda i, s: (i, 0)),
        scratch_shapes=[pltpu.VMEM((2, win, D // 2), jnp.uint32), pltpu.SemaphoreType.DMA((2,)),
                        pltpu.VMEM((tb, D), F32), pltpu.SMEM((1,), jnp.int32)],
    )
    return pl.pallas_call(
        kern, grid_spec=grid_spec, out_shape=jax.ShapeDtypeStruct((T, D), F32),
        compiler_params=_cparams(("arbitrary",)),
        name="combine",
    )(blk_start, h, off2d, cnt2d, y_sorted)


def _ffn_kernel(x_ref, gate_ref, wg_ref, wu_ref, wd_ref, o_ref, *, fc):
    x = _unpack_bf16_pairs(x_ref[0])
    F = wg_ref.shape[2]
    acc = jnp.zeros((x.shape[0], wd_ref.shape[2]), F32)
    for c in range(F // fc):
        cols = slice(c * fc, (c + 1) * fc)
        g = jnp.dot(x, wg_ref[0, :, cols], preferred_element_type=F32)
        u = jnp.dot(x, wu_ref[0, :, cols], preferred_element_type=F32)
        hid = (g * jax.nn.sigmoid(g) * u).astype(BF16)
        acc = acc + jnp.dot(hid, wd_ref[0, cols, :], preferred_element_type=F32)
    o_ref[0] = _pack_bf16_pairs(acc * gate_ref[0])


def _ffn(xe, gate, wg, wu, wd, tm, fc):
    E, cap, Dh = xe.shape
    D = 2 * Dh
    F = wg.shape[2]
    fc = min(fc, F)
    return pl.pallas_call(
        functools.partial(_ffn_kernel, fc=fc),
        grid=(E, cap // tm),
        in_specs=[pl.BlockSpec((1, tm, Dh), lambda e, i: (e, i, 0)),
                  pl.BlockSpec((1, tm, 1), lambda e, i: (e, i, 0)),
                  pl.BlockSpec((1, D, F), lambda e, i: (e, 0, 0)),
                  pl.BlockSpec((1, D, F), lambda e, i: (e, 0, 0)),
                  pl.BlockSpec((1, F, D), lambda e, i: (e, 0, 0))],
        out_specs=pl.BlockSpec((1, tm, Dh), lambda e, i: (e, i, 0)),
        out_shape=jax.ShapeDtypeStruct((E, cap, Dh), jnp.uint32),
        compiler_params=_cparams(("parallel", "arbitrary")),
        name="expert_ffn",
    )(xe, gate, wg, wu, wd)


def _final_norm_kernel(x_ref, g_ref, o_ref):
    x = x_ref[...]
    ms = jnp.mean(x * x, axis=-1, keepdims=True)
    o_ref[...] = x * lax.rsqrt(ms + EPS) * g_ref[...]


def _final_norm(x2d, g, tm):
    T, D = x2d.shape
    return pl.pallas_call(
        _final_norm_kernel,
        grid=(T // tm,),
        in_specs=[pl.BlockSpec((tm, D), lambda i: (i, 0)), pl.BlockSpec((1, D), lambda i: (0, 0))],
        out_specs=pl.BlockSpec((tm, D), lambda i: (i, 0)),
        out_shape=jax.ShapeDtypeStruct((T, D), F32),
        compiler_params=_cparams(("parallel",)),
        name="final_norm",
    )(x2d, g)


def _prep_layer(l, norm1, w_in, ln_v_g, ln_v_b, w_spatial, b_spatial, gla_decay_w, gla_decay_b, gla_norm,
                w_out, norm2, w_router, diff_subln):
    D = w_in.shape[1]
    w = jnp.pad(w_in[l], ((0, 0), (0, PROJ_PAD - PROJ_WIDTH))).astype(BF16)
    bs = jnp.repeat(b_spatial[l].T, B_GROUP_DIM, axis=1)
    wdec = jnp.zeros((128, 2 * C_KEY_WIDTH), F32)
    wdec = wdec.at[:C_DECAY_RANK, :C_KEY_WIDTH].set(gla_decay_w[l, 0])
    wdec = wdec.at[C_DECAY_RANK:2 * C_DECAY_RANK, C_KEY_WIDTH:].set(gla_decay_w[l, 1])
    bdec = gla_decay_b[l].reshape(1, 2 * C_KEY_WIDTH)
    wr = jnp.pad(w_router[l], ((0, 0), (0, 128 - N_EXPERTS)))
    wr_hi = wr.astype(BF16)
    wr_lo = (wr - wr_hi.astype(F32)).astype(BF16)
    return dict(
        g1=norm1[l].reshape(1, D), w=w, lng=ln_v_g[l].reshape(1, B_WIDTH), lnb=ln_v_b[l].reshape(1, B_WIDTH),
        ws=w_spatial[l].astype(BF16), bs=bs, wdec=wdec.astype(BF16), bdec=bdec,
        gn=jnp.tile(gla_norm[l], C_HEADS).reshape(1, C_WIDTH), w_out=w_out[l].astype(BF16),
        g2=norm2[l].reshape(1, D), wr_hi=wr_hi, wr_lo=wr_lo, subln=diff_subln[l].reshape(A_V_DIM, 1),
    )


def _tile(n, pref):
    t = min(n, pref)
    while n % t:
        t //= 2
    return t


def _trunk(x, layers, lam_vecs, ffn_w, norm_f):
    B, S, D = x.shape
    T = B * S
    cap = EC_CAPACITY_FACTOR * T // N_EXPERTS
    tm = _tile(T, 512)
    tk = _tile(S, 512)
    tq = min(tk, 256)
    kpos = _key_pos_lanes(tk)
    x2d = x.reshape(T, D)
    for l, p in enumerate(layers):
        lam_init = 0.8 - 0.6 * math.exp(-0.3 * l)
        aq, ak, avt, ob, cq, ck, cv, cg, laf, lab = _inproj(
            x2d, p["g1"], p["w"], p["lng"], p["lnb"], p["ws"], p["bs"], p["wdec"], p["bdec"], tm, min(tk, tm))
        r3 = lambda a: a.reshape(B, S, a.shape[-1])
        lq1, lk1, lq2, lk2 = (v[l:l + 1] for v in lam_vecs)
        oa = _attention(r3(aq), r3(ak), avt, lq1, lk1, lq2, lk2, p["subln"], kpos, lam_init, tq, tk)
        oc = _gla(r3(cq), r3(ck), r3(cv), r3(cg), r3(laf), r3(lab), p["gn"], _tile(S, 512))
        h, hn, aff_t = _outproj(x2d, oa.reshape(T, A_WIDTH), ob, oc.reshape(T, C_WIDTH),
                                p["w_out"], p["g2"], p["wr_hi"], p["wr_lo"], tm)
        sel, pos, dst, off2d, cnt2d = _route(aff_t, cap)
        flat = lambda a: a.reshape(N_EXPERTS * T)
        idx, gate, dstl = _sc_compact(flat(sel), flat(pos), flat(dst), flat(aff_t), N_EXPERTS, T, cap)
        xe = _sc_gather_rows(hn, idx)
        wg, wu, wd = (w[l] for w in ffn_w)
        ye = _ffn(xe.reshape(N_EXPERTS, cap, D // 2), gate.reshape(N_EXPERTS, cap, 1), wg, wu, wd,
                  _tile(cap, 512), 512)
        y_sorted = _sc_scatter_rows(ye.reshape(N_EXPERTS * cap, D // 2), dstl)
        tb = _tile(T, 1024)
        blk_start = jnp.concatenate([off2d.reshape(T)[::tb], jnp.full((1,), N_EXPERTS * cap, jnp.int32)])
        x2d = _combine(h, off2d, cnt2d, blk_start, y_sorted, tb, 512)
    return _final_norm(x2d, norm_f.reshape(1, D), tm).reshape(B, S, D)


def kernel(x_prompt, x_sample, norm1, w_in, lam_q1, lam_k1, lam_q2, lam_k2, diff_subln, ln_v_g, ln_v_b,
           w_spatial, b_spatial, gla_decay_w, gla_decay_b, gla_norm, w_out, norm2, w_router, w_gate, w_up,
           w_down, norm_f):
    depth = w_in.shape[0]
    layers = [_prep_layer(l, norm1, w_in, ln_v_g, ln_v_b, w_spatial, b_spatial, gla_decay_w, gla_decay_b,
                          gla_norm, w_out, norm2, w_router, diff_subln) for l in range(depth)]
    ffn_w = (w_gate.astype(BF16), w_up.astype(BF16), w_down.astype(BF16))
    lam_vecs = (lam_q1, lam_k1, lam_q2, lam_k2)
    y_prompt = _trunk(x_prompt, layers, lam_vecs, ffn_w, norm_f)
    y_sample = _trunk(x_sample, layers, lam_vecs, ffn_w, norm_f)
    return (y_prompt, y_sample)
```

```python
import functools
import math

import jax
import jax.numpy as jnp
from jax import lax
from jax.experimental import pallas as pl
from jax.experimental.pallas import tpu as pltpu
from jax.experimental.pallas import tpu_sc as plsc

F32 = jnp.float32
BF16 = jnp.bfloat16

EPS = 1e-6
LOG2E = 1.4426950408889634

A_HEADS = 4
A_QK_DIM = 64
A_V_DIM = 128
A_WIDTH = A_HEADS * A_V_DIM
B_GROUPS = 4
B_WIDTH = 256
B_GROUP_DIM = 64
B_CHUNK = 128
C_HEADS = 4
C_WIDTH = 256
C_V_DIM = 64
C_K_DIM = 32
C_KEY_WIDTH = 128
C_DECAY_RANK = 16
C_GATE_NORMALIZER = 16.0
C_CHUNK = 64
N_EXPERTS = 16
EC_CAPACITY_FACTOR = 2

OFF_AQ, OFF_AK, OFF_AV = 0, 512, 1024
OFF_BU, OFF_BV = 1536, 1792
OFF_CQ, OFF_CK, OFF_CV, OFF_CG, OFF_CZ = 2048, 2176, 2304, 2560, 2816
PROJ_WIDTH = 2848
PROJ_PAD = 2944

VMEM_LIMIT = 56 * 1024 * 1024


def _cparams(sem):
    return pltpu.CompilerParams(dimension_semantics=sem, vmem_limit_bytes=VMEM_LIMIT)


def _gelu_tanh(x):
    return 0.5 * x * (1.0 + jnp.tanh(0.7978845608028654 * (x + 0.044715 * x * x * x)))


def _log_sigmoid(x):
    return jnp.minimum(x, 0.0) - jnp.log(1.0 + jnp.exp(-jnp.abs(x)))


def _inproj_kernel(x_ref, g1_ref, w_ref, lng_ref, lnb_ref, ws_ref, bs_ref, wdec_ref, bdec_ref,
                   aq_ref, ak_ref, av_ref, ob_ref, cq_ref, ck_ref, cv_ref, cg_ref, laf_ref, lab_ref,
                   st_ref, *, tq):
    x = x_ref[...]
    ms = jnp.mean(x * x, axis=-1, keepdims=True)
    hn = (x * lax.rsqrt(ms + EPS) * g1_ref[...]).astype(BF16)
    proj = jnp.dot(hn, w_ref[...], preferred_element_type=F32)

    aq_ref[...] = (proj[:, OFF_AQ:OFF_AK] * (A_QK_DIM ** -0.5 * LOG2E)).astype(BF16)
    ak_ref[...] = proj[:, OFF_AK:OFF_AV].astype(BF16)
    tkb = av_ref.shape[2]
    for c in range(av_ref.shape[0]):
        av_ref[c] = jnp.transpose(proj[c * tkb:(c + 1) * tkb, OFF_AV:OFF_BU]).astype(BF16)
    qs = proj[:, OFF_AQ:OFF_AK] * (A_QK_DIM ** -0.5 * LOG2E)
    kf = proj[:, OFF_AK:OFF_AV]
    grp = (lax.broadcasted_iota(jnp.int32, (A_WIDTH, 128), 0) // A_QK_DIM
           == lax.broadcasted_iota(jnp.int32, (A_WIDTH, 128), 1)).astype(BF16)
    gsum = lambda a: jnp.dot(a.astype(BF16), grp, preferred_element_type=F32)
    qn2, kn2, ss = gsum(qs * qs), gsum(kf * kf), gsum(qs * kf)
    ntile = tkb // tq
    for c in range(av_ref.shape[0]):
        rows = [jnp.max(kn2[c * tkb:(c + 1) * tkb], axis=0, keepdims=True)]
        for t in range(ntile):
            lo = c * tkb + t * tq
            rows.append(jnp.max(qn2[lo:lo + tq], axis=0, keepdims=True))
        for t in range(ntile):
            lo = c * tkb + t * tq
            rows.append(jnp.min(ss[lo:lo + tq], axis=0, keepdims=True))
        rows += [jnp.zeros((1, 128), F32)] * (8 - len(rows))
        st_ref[c] = jnp.concatenate(rows, axis=0)

    u = _gelu_tanh(proj[:, OFF_BU:OFF_BV])
    v = _gelu_tanh(proj[:, OFF_BV:OFF_CQ])
    mu = jnp.mean(v, axis=-1, keepdims=True)
    var = jnp.mean(jnp.square(v - mu), axis=-1, keepdims=True)
    v = ((v - mu) * lax.rsqrt(var + EPS) * lng_ref[...] + lnb_ref[...]).astype(BF16)
    tm = x.shape[0]
    lane = lax.broadcasted_iota(jnp.int32, (B_CHUNK, 128), 1)
    first_half = lane < B_GROUP_DIM
    for c in range(tm // B_CHUNK):
        rows = slice(c * B_CHUNK, (c + 1) * B_CHUNK)
        parts = []
        for p in range(B_GROUPS // 2):
            vch = v[rows, p * 128:(p + 1) * 128]
            m0 = jnp.dot(ws_ref[2 * p], vch, preferred_element_type=F32)
            m1 = jnp.dot(ws_ref[2 * p + 1], vch, preferred_element_type=F32)
            parts.append(jnp.where(first_half, m0, m1))
        mixed = jnp.concatenate(parts, axis=1) + bs_ref[...]
        ob_ref[rows, :] = (u[rows, :] * mixed).astype(BF16)

    cq_ref[...] = proj[:, OFF_CQ:OFF_CK] * (C_K_DIM ** -0.5)
    ck_ref[...] = proj[:, OFF_CK:OFF_CV]
    cv_ref[...] = proj[:, OFF_CV:OFF_CG].astype(BF16)
    cg_ref[...] = proj[:, OFF_CG:OFF_CZ]
    z = proj[:, OFF_CZ:PROJ_PAD].astype(BF16)
    xd = jnp.dot(z, wdec_ref[...], preferred_element_type=F32) + bdec_ref[...]
    la = _log_sigmoid(xd) * (1.0 / C_GATE_NORMALIZER)
    laf_ref[...] = la[:, :C_KEY_WIDTH]
    lab_ref[...] = la[:, C_KEY_WIDTH:]


def _inproj(x2d, g1, w, lng, lnb, ws, bs, wdec, bdec, tm, tk, tq):
    T, D = x2d.shape
    row = lambda n: pl.BlockSpec((tm, n), lambda i: (i, 0))
    full = lambda a: pl.BlockSpec(a.shape, lambda i: (0,) * a.ndim)
    outs = [
        (A_WIDTH, BF16), (A_WIDTH, BF16), None, (B_WIDTH, BF16),
        (C_KEY_WIDTH, F32), (C_KEY_WIDTH, F32), (C_WIDTH, BF16), (C_WIDTH, F32),
        (C_KEY_WIDTH, F32), (C_KEY_WIDTH, F32),
    ]
    vt_spec = pl.BlockSpec((tm // tk, A_WIDTH, tk), lambda i: (i, 0, 0))
    vt_shape = jax.ShapeDtypeStruct((T // tk, A_WIDTH, tk), BF16)
    st_spec = pl.BlockSpec((tm // tk, 8, 128), lambda i: (i, 0, 0))
    st_shape = jax.ShapeDtypeStruct((T // tk, 8, 128), F32)
    return pl.pallas_call(
        functools.partial(_inproj_kernel, tq=tq),
        grid=(T // tm,),
        in_specs=[row(D), full(g1), full(w), full(lng), full(lnb), full(ws), full(bs), full(wdec), full(bdec)],
        out_specs=[vt_spec if o is None else row(o[0]) for o in outs] + [st_spec],
        out_shape=[vt_shape if o is None else jax.ShapeDtypeStruct((T, o[0]), o[1]) for o in outs] + [st_shape],
        compiler_params=_cparams(("parallel",)),
        name="inproj",
    )(x2d, g1, w, lng, lnb, ws, bs, wdec, bdec)


N_POS_LANES = 9


def _split3(x):
    hi = x.astype(BF16)
    r1 = x - hi.astype(F32)
    mid = r1.astype(BF16)
    lo = (r1 - mid.astype(F32)).astype(BF16)
    return hi, mid, lo


def _attn_kernel(lq1_ref, lk1_ref, lq2_ref, lk2_ref, g_ref, kpos_ref, st_ref, q_ref, k_ref, vt_ref, o_ref,
                 qall_ref, *chain_refs, tq, tk, nq, lam_init):
    nch = 2 * nq
    grp = lambda g: chain_refs[g * nch:(g + 1) * nch]
    m_refs, l_refs, acc_refs = grp(0), grp(1), grp(2)
    s_refs, p_refs, al_refs = (grp(3), grp(4)), (grp(5), grp(6)), (grp(7), grp(8))
    h = pl.program_id(1)
    qi = pl.program_id(2)
    nblk = k_ref.shape[1] // tk
    nrest = nblk - 1
    lam = (jnp.exp(jnp.sum(lq1_ref[...] * lk1_ref[...], axis=-1, keepdims=True))
           - jnp.exp(jnp.sum(lq2_ref[...] * lk2_ref[...], axis=-1, keepdims=True)) + lam_init)
    c = jnp.exp2(-8.0 * (h + 1).astype(F32) / A_HEADS) * LOG2E

    lane = lax.broadcasted_iota(jnp.int32, (tq, 128), 1)
    il = lax.broadcasted_iota(jnp.int32, (tq, 128), 0).astype(F32)
    hi, mid, lo = _split3(jnp.where(lane < 3, -c * il, c))
    piece = lane % 3
    qpos = jnp.where(piece == 0, hi, jnp.where(piece == 1, mid, lo))
    qpos = jnp.where(lane < N_POS_LANES, qpos, jnp.zeros_like(qpos))
    chains = [(t, m) for t in range(nq) for m in range(2)]
    for ci, (t, m) in enumerate(chains):
        q = q_ref[0, t * tq:(t + 1) * tq, :]
        keep = (lane < A_QK_DIM) if m == 0 else (lane >= A_QK_DIM)
        qm = jnp.where(keep, q, jnp.zeros_like(q))
        qall_ref[0, ci] = jnp.concatenate([qm, qpos], axis=1)
        qall_ref[1, ci] = jnp.concatenate([qm, -qpos], axis=1)
    kpos = kpos_ref[...]

    def block_of(i):
        above = (i >= qi).astype(jnp.int32)
        return i + above, above

    def scores(j, variant, ci):
        kaug = jnp.concatenate([k_ref[0, pl.ds(pl.multiple_of(j * tk, tk), tk), :], kpos], axis=1)
        return lax.dot_general(kaug, qall_ref[variant, ci], (((1,), (1,)), ((), ())),
                               preferred_element_type=F32)

    def softmax(j, ci, s, first):
        t, _ = chains[ci]
        off = ((qi * nq + t) * tq - j * tk).astype(F32)
        delta = c * jnp.abs(off)
        if first:
            rel = (lax.broadcasted_iota(jnp.int32, (tk, tq), 0)
                   - lax.broadcasted_iota(jnp.int32, (tk, tq), 1)).astype(F32)
            s = s - (2.0 * c) * jnp.maximum(rel - float(t * tq), 0.0)
        bmax = jnp.max(s, axis=0, keepdims=True) - delta
        if first:
            mn = bmax
        else:
            mx = m_refs[ci][...]
            mn = jnp.maximum(mx, bmax)
        p = jnp.exp2(s - (mn + delta))
        psum = jnp.sum(p, axis=0, keepdims=True)
        m_refs[ci][...] = mn
        if first:
            l_refs[ci][...] = psum
            return p.astype(BF16), None
        alpha = jnp.exp2(mx - mn)
        l_refs[ci][...] = alpha * l_refs[ci][...] + psum
        return p.astype(BF16), alpha

    n = nblk
    if n >= 4:
        st = st_ref[...]
        mine = st_ref[qi]
        jv = lax.broadcasted_iota(jnp.int32, (n, 128), 0)
        gl = lax.broadcasted_iota(jnp.int32, (n, 128), 1)
        ub = jnp.full((n, 128), -jnp.inf, F32)
        for t in range(nq):
            i0 = (qi * nq + t) * tq
            gap = jnp.where(jv < qi, i0 - (jv * tk + tk - 1), jv * tk - (i0 + tq - 1))
            dmin = jnp.maximum(gap, 0).astype(F32)
            bound = 1.1 * jnp.sqrt(mine[1 + t:2 + t, :] * st[:, 0, :]) - mine[1 + nq + t:2 + nq + t, :] - c * dmin
            ub = jnp.maximum(ub, bound)
        ub = jnp.where(gl // 2 == h, ub, -jnp.inf)
        live = jnp.logical_or(jnp.max(ub, axis=1, keepdims=True) > -150.0, jv[:, :1] == qi)
        j1 = jv[:, :1]
        jlo = jnp.min(jnp.where(live, j1, n))
        jhi = jnp.max(jnp.where(live, j1, -1))
        cnt = jhi - jlo
        need = jnp.maximum(cnt + 1 - cnt % 2, 3)
        grow_hi = jnp.minimum(need - cnt, (n - 1) - jhi)
        lo = jlo - (need - cnt - grow_hi)
        nvis = need + 1
    else:
        lo = 0
        nvis = n

    def visit(u):
        if isinstance(u, int) and u == 0:
            return qi, 0
        return block_of(lo + u - 1)

    def tick(tau, do_qk, do_sm, do_pv):
        tau, par = tau
        first_sm = isinstance(tau, int) and tau == 1
        first_pv = isinstance(tau, int) and tau == 2
        if do_qk:
            jq, vq = visit(tau)
        if do_sm:
            js, _ = visit(tau - 1)
        if do_pv:
            jp, _ = visit(tau - 2)
        for half in range(0, nch, 2):
            for ci in (half, half + 1):
                if do_qk:
                    s_refs[par][ci][...] = scores(jq, vq, ci)
            for ci in (half, half + 1):
                if do_sm:
                    p, alpha = softmax(js, ci, s_refs[1 - par][ci][...], first_sm)
                    p_refs[1 - par][ci][...] = p
                    if not first_sm:
                        al_refs[1 - par][ci][...] = alpha
            for ci in (half, half + 1):
                if do_pv:
                    pv = jnp.dot(vt_ref[jp], p_refs[par][ci][...], preferred_element_type=F32)
                    if first_pv:
                        acc_refs[ci][...] = pv
                    else:
                        acc_refs[ci][...] = al_refs[par][ci][...] * acc_refs[ci][...] + pv

    assert n == 1 or n % 2 == 0
    tick((0, 0), True, False, False)
    tick((1, 1), n > 1, True, False)
    if n == 1:
        tick((2, 0), False, False, True)
    else:
        tick((2, 0), n > 2, True, True)
        if n > 2:
            tick((3, 1), True, True, True)

            def pair(ip, carry):
                tau = 4 + 2 * ip
                tick((tau, 0), True, True, True)
                tick((tau + 1, 1), True, True, True)
                return carry

            lax.fori_loop(0, (nvis - 4) // 2, pair, 0)
            tick((nvis, 0), False, True, True)
        tick((nvis + 1, 1), False, False, True)

    for t in range(nq):
        a0, a1 = acc_refs[2 * t][...], acc_refs[2 * t + 1][...]
        o = a0 / l_refs[2 * t][...] - lam * (a1 / l_refs[2 * t + 1][...])
        ms = jnp.mean(o * o, axis=0, keepdims=True)
        o = o * lax.rsqrt(ms + EPS) * g_ref[...] * (1.0 - lam_init)
        o_ref[0, t * tq:(t + 1) * tq, :] = jnp.transpose(o).astype(o_ref.dtype)


def _attention(aq, ak, avt, stats, lq1, lk1, lq2, lk2, subln_col, kpos, lam_init, tq, tk):
    B, S, _ = aq.shape
    nq = tk // tq
    nch = 2 * nq
    assert S == tk or (S // tk) % 2 == 0
    vec = pl.BlockSpec((1, A_QK_DIM), lambda b, h, i: (0, 0))
    kern = functools.partial(_attn_kernel, tq=tq, tk=tk, nq=nq, lam_init=lam_init)
    return pl.pallas_call(
        kern,
        grid=(B, A_HEADS, S // tk),
        in_specs=[vec, vec, vec, vec,
                  pl.BlockSpec((A_V_DIM, 1), lambda b, h, i: (0, 0)),
                  pl.BlockSpec((tk, 128), lambda b, h, i: (0, 0)),
                  pl.BlockSpec((S // tk, 8, 128), lambda b, h, i: (b, 0, 0)),
                  pl.BlockSpec((1, tk, 128), lambda b, h, i: (b, i, h)),
                  pl.BlockSpec((1, S, 128), lambda b, h, i: (b, 0, h)),
                  pl.BlockSpec((S // tk, A_V_DIM, tk), lambda b, h, i: (b, h, 0))],
        out_specs=pl.BlockSpec((1, tk, 128), lambda b, h, i: (b, i, h)),
        out_shape=jax.ShapeDtypeStruct((B, S, A_WIDTH), BF16),
        scratch_shapes=([pltpu.VMEM((2, nch, tq, 256), BF16)]
                        + [pltpu.VMEM((1, tq), F32)] * (2 * nch)
                        + [pltpu.VMEM((A_V_DIM, tq), F32)] * nch
                        + [pltpu.VMEM((tk, tq), F32)] * (2 * nch)
                        + [pltpu.VMEM((tk, tq), BF16)] * (2 * nch)
                        + [pltpu.VMEM((1, tq), F32)] * (2 * nch)),
        compiler_params=_cparams(("parallel", "parallel", "arbitrary")),
        name="diff_attn",
    )(lq1, lk1, lq2, lk2, subln_col, kpos, stats, aq, ak, avt)


def _key_pos_lanes(tk):
    j = jnp.arange(tk, dtype=jnp.int32)
    jlo = (j % 256).astype(F32)
    jhi = (j - j % 256).astype(F32)
    cols = [jnp.ones((tk,), F32)] * 3 + [jlo] * 3 + [jhi] * 3
    kp = jnp.stack(cols, axis=1)
    return jnp.pad(kp, ((0, 0), (0, 128 - N_POS_LANES))).astype(BF16)


def _gla_chunk(q, k, v, la, st, reverse):
    C = C_CHUNK
    r = lax.broadcasted_iota(jnp.int32, (C, C), 0)
    c = lax.broadcasted_iota(jnp.int32, (C, C), 1)
    tri = (c >= r) if reverse else (c <= r)
    tri_b = tri.astype(BF16)
    la_hi = la.astype(BF16)
    la_lo = (la - la_hi.astype(F32)).astype(BF16)
    b = (jnp.dot(tri_b, la_hi, preferred_element_type=F32)
         + jnp.dot(tri_b, la_lo, preferred_element_type=F32))
    b_end = b[0:1, :] if reverse else b[C - 1:C, :]
    eb = jnp.exp(b)
    qt = (q * eb).astype(BF16)
    kt = k * jnp.exp(-b)
    kdec = (k * jnp.exp(b_end - b)).astype(BF16)

    lane_k = lax.broadcasted_iota(jnp.int32, (C, C_KEY_WIDTH), 1) // C_K_DIM
    kstack = jnp.concatenate([jnp.where(lane_k == hh, kt, 0.0) for hh in range(C_HEADS)], axis=0).astype(BF16)
    attn = lax.dot_general(qt, kstack, (((1,), (1,)), ((), ())), preferred_element_type=F32)
    ri = lax.broadcasted_iota(jnp.int32, (C, C_HEADS * C), 0)
    cj = lax.broadcasted_iota(jnp.int32, (C, C_HEADS * C), 1) % C
    keep = (cj >= ri) if reverse else (cj <= ri)
    attn = jnp.where(keep, attn, 0.0).astype(BF16)
    lane_v = lax.broadcasted_iota(jnp.int32, (C, C_WIDTH), 1) // C_V_DIM
    vstack = jnp.concatenate([jnp.where(lane_v == hh, v, jnp.zeros_like(v)) for hh in range(C_HEADS)], axis=0)
    o = jnp.dot(attn, vstack, preferred_element_type=F32)
    o = o + lax.dot_general(qt, st.astype(BF16), (((1,), (1,)), ((), ())), preferred_element_type=F32)

    upd = lax.dot_general(v, kdec, (((0,), (0,)), ((), ())), preferred_element_type=F32)
    rh = lax.broadcasted_iota(jnp.int32, (C_WIDTH, C_KEY_WIDTH), 0) // C_V_DIM
    ch = lax.broadcasted_iota(jnp.int32, (C_WIDTH, C_KEY_WIDTH), 1) // C_K_DIM
    st = st * jnp.exp(b_end) + jnp.where(rh == ch, upd, 0.0)
    return o, st


def _gla_fwd_kernel(q_ref, k_ref, v_ref, la_ref, o_ref, st_ref, *, nchunk):
    @pl.when(pl.program_id(1) == 0)
    def _():
        st_ref[...] = jnp.zeros_like(st_ref)

    st = st_ref[...]
    for c in range(nchunk):
        rows = slice(c * C_CHUNK, (c + 1) * C_CHUNK)
        o, st = _gla_chunk(q_ref[0, rows, :], k_ref[0, rows, :], v_ref[0, rows, :], la_ref[0, rows, :], st, False)
        o_ref[0, rows, :] = o
    st_ref[...] = st


def _gla_bwd_kernel(q_ref, k_ref, v_ref, la_ref, of_ref, g_ref, gn_ref, o_ref, st_ref, *, nchunk):
    @pl.when(pl.program_id(1) == 0)
    def _():
        st_ref[...] = jnp.zeros_like(st_ref)

    st = st_ref[...]
    lane_h = lax.broadcasted_iota(jnp.int32, (C_WIDTH, C_WIDTH), 0) // C_V_DIM
    lane_h2 = lax.broadcasted_iota(jnp.int32, (C_WIDTH, C_WIDTH), 1) // C_V_DIM
    seg = (lane_h == lane_h2).astype(F32) * (1.0 / C_V_DIM)
    for c in reversed(range(nchunk)):
        rows = slice(c * C_CHUNK, (c + 1) * C_CHUNK)
        o, st = _gla_chunk(q_ref[0, rows, :], k_ref[0, rows, :], v_ref[0, rows, :], la_ref[0, rows, :], st, True)
        o = o + of_ref[0, rows, :]
        sq = o * o
        sq_hi = sq.astype(BF16)
        sq_lo = (sq - sq_hi.astype(F32)).astype(BF16)
        segb = seg.astype(BF16)
        ms = (jnp.dot(sq_hi, segb, preferred_element_type=F32) + jnp.dot(sq_lo, segb, preferred_element_type=F32))
        o = o * lax.rsqrt(ms + EPS) * gn_ref[...]
        g = g_ref[0, rows, :]
        o_ref[0, rows, :] = (o * (g * jax.nn.sigmoid(g))).astype(o_ref.dtype)
    st_ref[...] = st


def _gla(cq, ck, cv, cg, laf, lab, gn_tiled, rb):
    B, S, _ = cq.shape
    nb = S // rb
    nchunk = rb // C_CHUNK
    fspec = lambda n: pl.BlockSpec((1, rb, n), lambda b, i: (b, i, 0))
    bspec = lambda n: pl.BlockSpec((1, rb, n), lambda b, i: (b, nb - 1 - i, 0))
    st = pltpu.VMEM((C_WIDTH, C_KEY_WIDTH), F32)
    o_f = pl.pallas_call(
        functools.partial(_gla_fwd_kernel, nchunk=nchunk),
        grid=(B, nb),
        in_specs=[fspec(C_KEY_WIDTH), fspec(C_KEY_WIDTH), fspec(C_WIDTH), fspec(C_KEY_WIDTH)],
        out_specs=fspec(C_WIDTH),
        out_shape=jax.ShapeDtypeStruct((B, S, C_WIDTH), F32),
        scratch_shapes=[st],
        compiler_params=_cparams(("parallel", "arbitrary")),
        name="gla_fwd",
    )(cq, ck, cv, laf)
    return pl.pallas_call(
        functools.partial(_gla_bwd_kernel, nchunk=nchunk),
        grid=(B, nb),
        in_specs=[bspec(C_KEY_WIDTH), bspec(C_KEY_WIDTH), bspec(C_WIDTH), bspec(C_KEY_WIDTH),
                  bspec(C_WIDTH), bspec(C_WIDTH), pl.BlockSpec((1, C_WIDTH), lambda b, i: (0, 0))],
        out_specs=bspec(C_WIDTH),
        out_shape=jax.ShapeDtypeStruct((B, S, C_WIDTH), BF16),
        scratch_shapes=[st],
        compiler_params=_cparams(("parallel", "arbitrary")),
        name="gla_bwd",
    )(cq, ck, cv, lab, o_f, cg, gn_tiled)


def _outproj_kernel(x_ref, oa_ref, ob_ref, oc_ref, w_ref, g2_ref, wr_hi_ref, wr_lo_ref,
                    h_ref, hn_ref, aff_ref):
    mix = jnp.concatenate([oa_ref[...], ob_ref[...], oc_ref[...]], axis=1)
    h = x_ref[...] + jnp.dot(mix, w_ref[...], preferred_element_type=F32)
    h_ref[...] = h
    ms = jnp.mean(h * h, axis=-1, keepdims=True)
    hn = h * lax.rsqrt(ms + EPS) * g2_ref[...]
    hn_ref[...] = _pack_bf16_pairs(hn)
    hn_hi = hn.astype(BF16)
    hn_lo = (hn - hn_hi.astype(F32)).astype(BF16)
    logits = (jnp.dot(hn_hi, wr_hi_ref[...], preferred_element_type=F32)
              + jnp.dot(hn_hi, wr_lo_ref[...], preferred_element_type=F32)
              + jnp.dot(hn_lo, wr_hi_ref[...], preferred_element_type=F32))
    lane = lax.broadcasted_iota(jnp.int32, logits.shape, 1)
    logits = jnp.where(lane < N_EXPERTS, logits, -jnp.inf)
    mx = jnp.max(logits, axis=-1, keepdims=True)
    e = jnp.exp(logits - mx)
    aff = e / jnp.sum(e, axis=-1, keepdims=True)
    aff_ref[...] = jnp.transpose(aff)[:N_EXPERTS, :]


def _outproj(x2d, oa, ob, oc, w, g2, wr_hi, wr_lo, tm):
    T, D = x2d.shape
    row = lambda n: pl.BlockSpec((tm, n), lambda i: (i, 0))
    full = lambda a: pl.BlockSpec(a.shape, lambda i: (0,) * a.ndim)
    return pl.pallas_call(
        _outproj_kernel,
        grid=(T // tm,),
        in_specs=[row(D), row(A_WIDTH), row(B_WIDTH), row(C_WIDTH), full(w), full(g2), full(wr_hi), full(wr_lo)],
        out_specs=[row(D), row(D // 2), pl.BlockSpec((N_EXPERTS, tm), lambda i: (0, i))],
        out_shape=[jax.ShapeDtypeStruct((T, D), F32), jax.ShapeDtypeStruct((T, D // 2), jnp.uint32),
                   jax.ShapeDtypeStruct((N_EXPERTS, T), F32)],
        compiler_params=_cparams(("parallel",)),
        name="outproj_router",
    )(x2d, oa, ob, oc, w, g2, wr_hi, wr_lo)


def _cumsum_tokens(x, upper, lstrict, ones):
    G, R, _ = x.shape
    xb = x.reshape(G * R, 128).astype(BF16)
    within = jnp.dot(xb, upper, preferred_element_type=F32).reshape(G, R, 128)
    rowtot = jnp.dot(xb, ones, preferred_element_type=F32).reshape(G, R, 128)
    hi = jnp.floor(rowtot * (1.0 / 256.0))
    lo = rowtot - 256.0 * hi
    outs = []
    for g in range(G):
        before = (256.0 * jnp.dot(lstrict, hi[g].astype(BF16), preferred_element_type=F32)
                  + jnp.dot(lstrict, lo[g].astype(BF16), preferred_element_type=F32))
        outs.append(within[g] + before)
    return jnp.stack(outs, axis=0)


def _route_kernel(aff_ref, sel_ref, pos_ref, dst_ref, off_ref, cnt_ref, *, cap):
    E, R, _ = aff_ref.shape
    bits = pltpu.bitcast(aff_ref[...].reshape(E * R, 128), jnp.int32).reshape(E, R, 128)

    def count_ge(v):
        return jnp.sum(jnp.sum((bits >= v).astype(jnp.int32), axis=1, keepdims=True), axis=2, keepdims=True)

    def bisect(i, prefix):
        cand = prefix | jnp.left_shift(jnp.int32(1), 30 - i)
        return jnp.where(count_ge(cand) >= cap, cand, prefix)

    thr = lax.fori_loop(0, 31, bisect, jnp.zeros((E, 1, 1), jnp.int32))
    gt = bits > thr
    eq = bits == thr
    n_gt = jnp.sum(jnp.sum(gt.astype(jnp.int32), axis=1, keepdims=True), axis=2, keepdims=True)
    need_eq = (cap - n_gt).astype(F32)

    r = lax.broadcasted_iota(jnp.int32, (128, 128), 0)
    cc = lax.broadcasted_iota(jnp.int32, (128, 128), 1)
    upper = (r <= cc).astype(BF16)
    ones = jnp.ones((128, 128), BF16)
    rr = lax.broadcasted_iota(jnp.int32, (R, R), 0)
    rc = lax.broadcasted_iota(jnp.int32, (R, R), 1)
    lstrict = (rc < rr).astype(BF16)

    eqf = eq.astype(F32)
    eq_rank = _cumsum_tokens(eqf, upper, lstrict, ones) - eqf
    sel = jnp.logical_or(gt, jnp.logical_and(eq, eq_rank < need_eq))
    self_ = sel.astype(F32)
    pos = _cumsum_tokens(self_, upper, lstrict, ones) - self_
    cnt = jnp.sum(self_, axis=0, keepdims=True)
    off = _cumsum_tokens(cnt, upper, lstrict, ones) - cnt
    rank = jnp.zeros((R, 128), F32)
    for e in range(E):
        dst_ref[e] = (off[0] + rank).astype(jnp.int32)
        rank = rank + self_[e]
    sel_ref[...] = sel.astype(jnp.int32)
    pos_ref[...] = pos.astype(jnp.int32)
    off_ref[...] = off[0].astype(jnp.int32)
    cnt_ref[...] = cnt[0].astype(jnp.int32)


def _route(aff_t, cap):
    E, T = aff_t.shape
    R = T // 128
    a3 = aff_t.reshape(E, R, 128)
    i3 = jax.ShapeDtypeStruct((E, R, 128), jnp.int32)
    i2 = jax.ShapeDtypeStruct((R, 128), jnp.int32)
    return pl.pallas_call(
        functools.partial(_route_kernel, cap=cap),
        out_shape=[i3, i3, i3, i2, i2],
        compiler_params=pltpu.CompilerParams(vmem_limit_bytes=VMEM_LIMIT),
        name="route_select",
    )(a3)


SC_CHUNK = 2048
SC_ROWS = 64


def _sc_mesh():
    return plsc.VectorSubcoreMesh(core_axis_name="c", subcore_axis_name="s")


def _sc_worker():
    info = pltpu.get_tpu_info().sparse_core
    return lax.axis_index("s") * info.num_cores + lax.axis_index("c"), info.num_cores * info.num_subcores


def _sc_compact(sel, pos, dst, aff, E, T, cap):
    @functools.partial(
        pl.kernel, mesh=_sc_mesh(),
        out_type=[jax.ShapeDtypeStruct((E * cap,), jnp.int32), jax.ShapeDtypeStruct((E * cap,), F32),
                  jax.ShapeDtypeStruct((E * cap,), jnp.int32)],
        scratch_types=[pltpu.VMEM((SC_CHUNK,), jnp.int32), pltpu.VMEM((SC_CHUNK,), jnp.int32),
                       pltpu.VMEM((SC_CHUNK,), jnp.int32), pltpu.VMEM((SC_CHUNK,), F32),
                       pltpu.VMEM((cap,), jnp.int32), pltpu.VMEM((cap,), F32), pltpu.VMEM((cap,), jnp.int32)],
        compiler_params=pltpu.CompilerParams(needs_layout_passes=False),
    )
    def k(sel_hbm, pos_hbm, dst_hbm, aff_hbm, idx_hbm, gate_hbm, dstl_hbm,
          sel_v, pos_v, dst_v, aff_v, idx_b, gate_b, dstl_b):
        e, _ = _sc_worker()

        @pl.when(e < E)
        def _():
            @pl.loop(0, T // SC_CHUNK)
            def _(ch):
                base = pl.multiple_of(e * T + ch * SC_CHUNK, 8)
                pltpu.sync_copy(sel_hbm.at[pl.ds(base, SC_CHUNK)], sel_v)
                pltpu.sync_copy(pos_hbm.at[pl.ds(base, SC_CHUNK)], pos_v)
                pltpu.sync_copy(dst_hbm.at[pl.ds(base, SC_CHUNK)], dst_v)
                pltpu.sync_copy(aff_hbm.at[pl.ds(base, SC_CHUNK)], aff_v)

                @pl.loop(0, SC_CHUNK // 16)
                def _(i):
                    sl = pl.ds(pl.multiple_of(i * 16, 16), 16)
                    chosen = sel_v[sl] > 0
                    slot = pos_v[sl]
                    tok = ch * SC_CHUNK + i * 16 + lax.iota(jnp.int32, 16)
                    plsc.store_scatter(idx_b, [slot], tok, mask=chosen)
                    plsc.store_scatter(gate_b, [slot], aff_v[sl], mask=chosen)
                    plsc.store_scatter(dstl_b, [slot], dst_v[sl], mask=chosen)

            out = pl.ds(pl.multiple_of(e * cap, 8), cap)
            pltpu.sync_copy(idx_b, idx_hbm.at[out])
            pltpu.sync_copy(gate_b, gate_hbm.at[out])
            pltpu.sync_copy(dstl_b, dstl_hbm.at[out])

    return k(sel, pos, dst, aff)


def _sc_gather_rows(table, idx):
    B, (_, D) = idx.shape[0], table.shape

    @functools.partial(
        pl.kernel, mesh=_sc_mesh(), out_type=jax.ShapeDtypeStruct((B, D), table.dtype),
        scratch_types=[pltpu.VMEM((SC_ROWS,), jnp.int32), pltpu.VMEM((SC_ROWS, D), table.dtype),
                       pltpu.SemaphoreType.DMA])
    def k(table_hbm, idx_hbm, out_hbm, idx_v, rows_v, sem):
        wid, nw = _sc_worker()
        per = B // nw

        @pl.loop(0, per // SC_ROWS)
        def _(g):
            off = pl.multiple_of(wid * per + g * SC_ROWS, 8)
            pltpu.sync_copy(idx_hbm.at[pl.ds(off, SC_ROWS)], idx_v)
            pltpu.async_copy(table_hbm.at[idx_v], rows_v, sem).wait()
            pltpu.sync_copy(rows_v, out_hbm.at[pl.ds(off, SC_ROWS)])

    return k(table, idx)


def _sc_scatter_rows(rows, dst):
    B, D = rows.shape

    @functools.partial(
        pl.kernel, mesh=_sc_mesh(), out_type=jax.ShapeDtypeStruct((B, D), rows.dtype),
        scratch_types=[pltpu.VMEM((SC_ROWS,), jnp.int32), pltpu.VMEM((SC_ROWS, D), rows.dtype),
                       pltpu.SemaphoreType.DMA])
    def k(rows_hbm, dst_hbm, out_hbm, dst_v, rows_v, sem):
        wid, nw = _sc_worker()
        per = B // nw

        @pl.loop(0, per // SC_ROWS)
        def _(g):
            off = pl.multiple_of(wid * per + g * SC_ROWS, 8)
            pltpu.sync_copy(dst_hbm.at[pl.ds(off, SC_ROWS)], dst_v)
            pltpu.sync_copy(rows_hbm.at[pl.ds(off, SC_ROWS)], rows_v)
            pltpu.async_copy(rows_v, out_hbm.at[dst_v], sem).wait()

    return k(rows, dst)


def _pack_bf16_pairs(x):
    m = x.shape[1] // 2
    bits = pltpu.bitcast(x.astype(BF16).astype(F32), jnp.uint32)
    return (bits[:, m:] & jnp.uint32(0xFFFF0000)) | (bits[:, :m] >> 16)


def _unpack_bf16_pairs(u):
    lo = pltpu.bitcast(u << 16, F32)
    hi = pltpu.bitcast(u & jnp.uint32(0xFFFF0000), F32)
    return jnp.concatenate([lo, hi], axis=1).astype(BF16)


def _combine_kernel(start_ref, h_ref, off_ref, cnt_ref, y_hbm, o_ref, ybuf, sem, acc_ref, nwin_done,
                    *, tb, win, nrows):
    b = pl.program_id(0)
    nb = pl.num_programs(0)

    def window_start(blk, w):
        a8 = (start_ref[blk] // 8) * 8
        return pl.multiple_of(jnp.minimum(a8 + w * win, nrows - win), 8)

    def n_windows(blk):
        a8 = (start_ref[blk] // 8) * 8
        return jnp.maximum((start_ref[blk + 1] - a8 + win - 1) // win, 1)

    def copy(blk, w, slot):
        return pltpu.make_async_copy(y_hbm.at[pl.ds(window_start(blk, w), win)], ybuf.at[slot], sem.at[slot])

    @pl.when(b == 0)
    def _():
        nwin_done[0] = 0
        copy(0, 0, 0).start()

    rows = off_ref.shape[0]
    off_row = jnp.concatenate([off_ref[i:i + 1, :] for i in range(rows)], axis=1)
    end_row = off_row + jnp.concatenate([cnt_ref[i:i + 1, :] for i in range(rows)], axis=1)
    acc_ref[...] = jnp.zeros_like(acc_ref)
    nwin = n_windows(b)

    def body(w, carry):
        g = nwin_done[0]
        slot = g % 2
        copy(b, w, slot).wait()

        @pl.when(w + 1 < nwin)
        def _():
            copy(b, w + 1, 1 - slot).start()

        @pl.when(jnp.logical_and(w + 1 == nwin, b + 1 < nb))
        def _():
            copy(b + 1, 0, 1 - slot).start()

        first_new = (start_ref[b] // 8) * 8 + w * win
        r = window_start(b, w) + lax.broadcasted_iota(jnp.int32, (win, tb), 0)
        onehot_t = jnp.logical_and(jnp.logical_and(r >= off_row, r < end_row), r >= first_new)
        onehot_t = jnp.where(onehot_t, 1.0, 0.0).astype(BF16)
        y = _unpack_bf16_pairs(ybuf[slot])
        acc_ref[...] += lax.dot_general(onehot_t, y, (((0,), (0,)), ((), ())), preferred_element_type=F32)
        nwin_done[0] = g + 1
        return carry

    lax.fori_loop(0, nwin, body, 0)
    o_ref[...] = h_ref[...] + acc_ref[...]


def _combine(h, off2d, cnt2d, blk_start, y_sorted, tb, win):
    T, D = h.shape
    nrows = y_sorted.shape[0]
    win = min(win, nrows)
    kern = functools.partial(_combine_kernel, tb=tb, win=win, nrows=nrows)
    grid_spec = pltpu.PrefetchScalarGridSpec(
        num_scalar_prefetch=1,
        grid=(T // tb,),
        in_specs=[pl.BlockSpec((tb, D), lambda i, s: (i, 0)),
                  pl.BlockSpec((tb // 128, 128), lambda i, s: (i, 0)),
                  pl.BlockSpec((tb // 128, 128), lambda i, s: (i, 0)),
                  pl.BlockSpec(memory_space=pl.ANY)],
        out_specs=pl.BlockSpec((tb, D), lambda i, s: (i, 0)),
        scratch_shapes=[pltpu.VMEM((2, win, D // 2), jnp.uint32), pltpu.SemaphoreType.DMA((2,)),
                        pltpu.VMEM((tb, D), F32), pltpu.SMEM((1,), jnp.int32)],
    )
    return pl.pallas_call(
        kern, grid_spec=grid_spec, out_shape=jax.ShapeDtypeStruct((T, D), F32),
        compiler_params=_cparams(("arbitrary",)),
        name="combine",
    )(blk_start, h, off2d, cnt2d, y_sorted)


def _ffn_kernel(x_ref, gate_ref, wg_ref, wu_ref, wd_ref, o_ref, *, fc):
    x = _unpack_bf16_pairs(x_ref[0])
    F = wg_ref.shape[2]
    acc = jnp.zeros((x.shape[0], wd_ref.shape[2]), F32)
    for c in range(F // fc):
        cols = slice(c * fc, (c + 1) * fc)
        g = jnp.dot(x, wg_ref[0, :, cols], preferred_element_type=F32)
        u = jnp.dot(x, wu_ref[0, :, cols], preferred_element_type=F32)
        hid = (g * jax.nn.sigmoid(g) * u).astype(BF16)
        acc = acc + jnp.dot(hid, wd_ref[0, cols, :], preferred_element_type=F32)
    o_ref[0] = _pack_bf16_pairs(acc * gate_ref[0])


def _ffn(xe, gate, wg, wu, wd, tm, fc):
    E, cap, Dh = xe.shape
    D = 2 * Dh
    F = wg.shape[2]
    fc = min(fc, F)
    return pl.pallas_call(
        functools.partial(_ffn_kernel, fc=fc),
        grid=(E, cap // tm),
        in_specs=[pl.BlockSpec((1, tm, Dh), lambda e, i: (e, i, 0)),
                  pl.BlockSpec((1, tm, 1), lambda e, i: (e, i, 0)),
                  pl.BlockSpec((1, D, F), lambda e, i: (e, 0, 0)),
                  pl.BlockSpec((1, D, F), lambda e, i: (e, 0, 0)),
                  pl.BlockSpec((1, F, D), lambda e, i: (e, 0, 0))],
        out_specs=pl.BlockSpec((1, tm, Dh), lambda e, i: (e, i, 0)),
        out_shape=jax.ShapeDtypeStruct((E, cap, Dh), jnp.uint32),
        compiler_params=_cparams(("parallel", "arbitrary")),
        name="expert_ffn",
    )(xe, gate, wg, wu, wd)


def _final_norm_kernel(x_ref, g_ref, o_ref):
    x = x_ref[...]
    ms = jnp.mean(x * x, axis=-1, keepdims=True)
    o_ref[...] = x * lax.rsqrt(ms + EPS) * g_ref[...]


def _final_norm(x2d, g, tm):
    T, D = x2d.shape
    return pl.pallas_call(
        _final_norm_kernel,
        grid=(T // tm,),
        in_specs=[pl.BlockSpec((tm, D), lambda i: (i, 0)), pl.BlockSpec((1, D), lambda i: (0, 0))],
        out_specs=pl.BlockSpec((tm, D), lambda i: (i, 0)),
        out_shape=jax.ShapeDtypeStruct((T, D), F32),
        compiler_params=_cparams(("parallel",)),
        name="final_norm",
    )(x2d, g)


def _prep_layer(l, norm1, w_in, ln_v_g, ln_v_b, w_spatial, b_spatial, gla_decay_w, gla_decay_b, gla_norm,
                w_out, norm2, w_router, diff_subln):
    D = w_in.shape[1]
    w = jnp.pad(w_in[l], ((0, 0), (0, PROJ_PAD - PROJ_WIDTH))).astype(BF16)
    bs = jnp.repeat(b_spatial[l].T, B_GROUP_DIM, axis=1)
    wdec = jnp.zeros((128, 2 * C_KEY_WIDTH), F32)
    wdec = wdec.at[:C_DECAY_RANK, :C_KEY_WIDTH].set(gla_decay_w[l, 0])
    wdec = wdec.at[C_DECAY_RANK:2 * C_DECAY_RANK, C_KEY_WIDTH:].set(gla_decay_w[l, 1])
    bdec = gla_decay_b[l].reshape(1, 2 * C_KEY_WIDTH)
    wr = jnp.pad(w_router[l], ((0, 0), (0, 128 - N_EXPERTS)))
    wr_hi = wr.astype(BF16)
    wr_lo = (wr - wr_hi.astype(F32)).astype(BF16)
    return dict(
        g1=norm1[l].reshape(1, D), w=w, lng=ln_v_g[l].reshape(1, B_WIDTH), lnb=ln_v_b[l].reshape(1, B_WIDTH),
        ws=w_spatial[l].astype(BF16), bs=bs, wdec=wdec.astype(BF16), bdec=bdec,
        gn=jnp.tile(gla_norm[l], C_HEADS).reshape(1, C_WIDTH), w_out=w_out[l].astype(BF16),
        g2=norm2[l].reshape(1, D), wr_hi=wr_hi, wr_lo=wr_lo, subln=diff_subln[l].reshape(A_V_DIM, 1),
    )


def _tile(n, pref):
    t = min(n, pref)
    while n % t:
        t //= 2
    return t


def _trunk(x, layers, lam_vecs, ffn_w, norm_f):
    B, S, D = x.shape
    T = B * S
    cap = EC_CAPACITY_FACTOR * T // N_EXPERTS
    tm = _tile(T, 512)
    tk = _tile(S, 512)
    tq = min(tk, 256)
    kpos = _key_pos_lanes(tk)
    x2d = x.reshape(T, D)
    for l, p in enumerate(layers):
        lam_init = 0.8 - 0.6 * math.exp(-0.3 * l)
        aq, ak, avt, ob, cq, ck, cv, cg, laf, lab, stats = _inproj(
            x2d, p["g1"], p["w"], p["lng"], p["lnb"], p["ws"], p["bs"], p["wdec"], p["bdec"], tm, min(tk, tm), tq)
        r3 = lambda a: a.reshape(B, S, a.shape[-1])
        lq1, lk1, lq2, lk2 = (v[l:l + 1] for v in lam_vecs)
        oa = _attention(r3(aq), r3(ak), avt, stats, lq1, lk1, lq2, lk2, p["subln"], kpos, lam_init, tq, tk)
        oc = _gla(r3(cq), r3(ck), r3(cv), r3(cg), r3(laf), r3(lab), p["gn"], _tile(S, 512))
        h, hn, aff_t = _outproj(x2d, oa.reshape(T, A_WIDTH), ob, oc.reshape(T, C_WIDTH),
                                p["w_out"], p["g2"], p["wr_hi"], p["wr_lo"], tm)
        sel, pos, dst, off2d, cnt2d = _route(aff_t, cap)
        flat = lambda a: a.reshape(N_EXPERTS * T)
        idx, gate, dstl = _sc_compact(flat(sel), flat(pos), flat(dst), flat(aff_t), N_EXPERTS, T, cap)
        xe = _sc_gather_rows(hn, idx)
        wg, wu, wd = (w[l] for w in ffn_w)
        ye = _ffn(xe.reshape(N_EXPERTS, cap, D // 2), gate.reshape(N_EXPERTS, cap, 1), wg, wu, wd,
                  _tile(cap, 512), 512)
        y_sorted = _sc_scatter_rows(ye.reshape(N_EXPERTS * cap, D // 2), dstl)
        tb = _tile(T, 1024)
        blk_start = jnp.concatenate([off2d.reshape(T)[::tb], jnp.full((1,), N_EXPERTS * cap, jnp.int32)])
        x2d = _combine(h, off2d, cnt2d, blk_start, y_sorted, tb, 512)
    return _final_norm(x2d, norm_f.reshape(1, D), tm).reshape(B, S, D)


def kernel(x_prompt, x_sample, norm1, w_in, lam_q1, lam_k1, lam_q2, lam_k2, diff_subln, ln_v_g, ln_v_b,
           w_spatial, b_spatial, gla_decay_w, gla_decay_b, gla_norm, w_out, norm2, w_router, w_gate, w_up,
           w_down, norm_f):
    depth = w_in.shape[0]
    layers = [_prep_layer(l, norm1, w_in, ln_v_g, ln_v_b, w_spatial, b_spatial, gla_decay_w, gla_decay_b,
                          gla_norm, w_out, norm2, w_router, diff_subln) for l in range(depth)]
    ffn_w = (w_gate.astype(BF16), w_up.astype(BF16), w_down.astype(BF16))
    lam_vecs = (lam_q1, lam_k1, lam_q2, lam_k2)
    y_prompt = _trunk(x_prompt, layers, lam_vecs, ffn_w, norm_f)
    y_sample = _trunk(x_sample, layers, lam_vecs, ffn_w, norm_f)
    return (y_prompt, y_sample)
```

```python
import functools
import math

import jax
import jax.numpy as jnp
from jax import lax
from jax.experimental import pallas as pl
from jax.experimental.pallas import tpu as pltpu
from jax.experimental.pallas import tpu_sc as plsc

F32 = jnp.float32
BF16 = jnp.bfloat16

EPS = 1e-6
LOG2E = 1.4426950408889634

A_HEADS = 4
A_QK_DIM = 64
A_V_DIM = 128
A_WIDTH = A_HEADS * A_V_DIM
B_GROUPS = 4
B_WIDTH = 256
B_GROUP_DIM = 64
B_CHUNK = 128
C_HEADS = 4
C_WIDTH = 256
C_V_DIM = 64
C_K_DIM = 32
C_KEY_WIDTH = 128
C_DECAY_RANK = 16
C_GATE_NORMALIZER = 16.0
C_CHUNK = 64
N_EXPERTS = 16
EC_CAPACITY_FACTOR = 2

OFF_AQ, OFF_AK, OFF_AV = 0, 512, 1024
OFF_BU, OFF_BV = 1536, 1792
OFF_CQ, OFF_CK, OFF_CV, OFF_CG, OFF_CZ = 2048, 2176, 2304, 2560, 2816
PROJ_WIDTH = 2848
PROJ_PAD = 2944

VMEM_LIMIT = 56 * 1024 * 1024


def _cparams(sem):
    return pltpu.CompilerParams(dimension_semantics=sem, vmem_limit_bytes=VMEM_LIMIT)


def _gelu_tanh(x):
    return 0.5 * x * (1.0 + jnp.tanh(0.7978845608028654 * (x + 0.044715 * x * x * x)))


def _log_sigmoid(x):
    return jnp.minimum(x, 0.0) - jnp.log(1.0 + jnp.exp(-jnp.abs(x)))


def _inproj_kernel(x_ref, g1_ref, w_ref, lng_ref, lnb_ref, ws_ref, bs_ref, wdec_ref, bdec_ref,
                   aq_ref, ak_ref, av_ref, ob_ref, cq_ref, ck_ref, cv_ref, cg_ref, laf_ref, lab_ref,
                   st_ref, *, tq):
    x = x_ref[...]
    ms = jnp.mean(x * x, axis=-1, keepdims=True)
    hn = (x * lax.rsqrt(ms + EPS) * g1_ref[...]).astype(BF16)
    proj = jnp.dot(hn, w_ref[...], preferred_element_type=F32)

    aq_ref[...] = (proj[:, OFF_AQ:OFF_AK] * (A_QK_DIM ** -0.5 * LOG2E)).astype(BF16)
    ak_ref[...] = proj[:, OFF_AK:OFF_AV].astype(BF16)
    tkb = av_ref.shape[2]
    for c in range(av_ref.shape[0]):
        av_ref[c] = jnp.transpose(proj[c * tkb:(c + 1) * tkb, OFF_AV:OFF_BU]).astype(BF16)
    qs = proj[:, OFF_AQ:OFF_AK] * (A_QK_DIM ** -0.5 * LOG2E)
    kf = proj[:, OFF_AK:OFF_AV]
    grp = (lax.broadcasted_iota(jnp.int32, (A_WIDTH, 128), 0) // A_QK_DIM
           == lax.broadcasted_iota(jnp.int32, (A_WIDTH, 128), 1)).astype(BF16)
    gsum = lambda a: jnp.dot(a.astype(BF16), grp, preferred_element_type=F32)
    qn2, kn2, ss = gsum(qs * qs), gsum(kf * kf), gsum(qs * kf)
    ntile = tkb // tq
    for c in range(av_ref.shape[0]):
        rows = [jnp.max(kn2[c * tkb:(c + 1) * tkb], axis=0, keepdims=True)]
        for t in range(ntile):
            lo = c * tkb + t * tq
            rows.append(jnp.max(qn2[lo:lo + tq], axis=0, keepdims=True))
        for t in range(ntile):
            lo = c * tkb + t * tq
            rows.append(jnp.min(ss[lo:lo + tq], axis=0, keepdims=True))
        rows += [jnp.zeros((1, 128), F32)] * (8 - len(rows))
        st_ref[c] = jnp.concatenate(rows, axis=0)

    u = _gelu_tanh(proj[:, OFF_BU:OFF_BV])
    v = _gelu_tanh(proj[:, OFF_BV:OFF_CQ])
    mu = jnp.mean(v, axis=-1, keepdims=True)
    var = jnp.mean(jnp.square(v - mu), axis=-1, keepdims=True)
    v = ((v - mu) * lax.rsqrt(var + EPS) * lng_ref[...] + lnb_ref[...]).astype(BF16)
    tm = x.shape[0]
    lane = lax.broadcasted_iota(jnp.int32, (B_CHUNK, 128), 1)
    first_half = lane < B_GROUP_DIM
    for c in range(tm // B_CHUNK):
        rows = slice(c * B_CHUNK, (c + 1) * B_CHUNK)
        parts = []
        for p in range(B_GROUPS // 2):
            vch = v[rows, p * 128:(p + 1) * 128]
            m0 = jnp.dot(ws_ref[2 * p], vch, preferred_element_type=F32)
            m1 = jnp.dot(ws_ref[2 * p + 1], vch, preferred_element_type=F32)
            parts.append(jnp.where(first_half, m0, m1))
        mixed = jnp.concatenate(parts, axis=1) + bs_ref[...]
        ob_ref[rows, :] = (u[rows, :] * mixed).astype(BF16)

    cq_ref[...] = proj[:, OFF_CQ:OFF_CK] * (C_K_DIM ** -0.5)
    ck_ref[...] = proj[:, OFF_CK:OFF_CV]
    cv_ref[...] = proj[:, OFF_CV:OFF_CG].astype(BF16)
    cg_ref[...] = proj[:, OFF_CG:OFF_CZ]
    z = proj[:, OFF_CZ:PROJ_PAD].astype(BF16)
    xd = jnp.dot(z, wdec_ref[...], preferred_element_type=F32) + bdec_ref[...]
    la = _log_sigmoid(xd) * (1.0 / C_GATE_NORMALIZER)
    laf_ref[...] = la[:, :C_KEY_WIDTH]
    lab_ref[...] = la[:, C_KEY_WIDTH:]


def _inproj(x2d, g1, w, lng, lnb, ws, bs, wdec, bdec, tm, tk, tq):
    T, D = x2d.shape
    row = lambda n: pl.BlockSpec((tm, n), lambda i: (i, 0))
    full = lambda a: pl.BlockSpec(a.shape, lambda i: (0,) * a.ndim)
    outs = [
        (A_WIDTH, BF16), (A_WIDTH, BF16), None, (B_WIDTH, BF16),
        (C_KEY_WIDTH, F32), (C_KEY_WIDTH, F32), (C_WIDTH, BF16), (C_WIDTH, F32),
        (C_KEY_WIDTH, F32), (C_KEY_WIDTH, F32),
    ]
    vt_spec = pl.BlockSpec((tm // tk, A_WIDTH, tk), lambda i: (i, 0, 0))
    vt_shape = jax.ShapeDtypeStruct((T // tk, A_WIDTH, tk), BF16)
    st_spec = pl.BlockSpec((tm // tk, 8, 128), lambda i: (i, 0, 0))
    st_shape = jax.ShapeDtypeStruct((T // tk, 8, 128), F32)
    return pl.pallas_call(
        functools.partial(_inproj_kernel, tq=tq),
        grid=(T // tm,),
        in_specs=[row(D), full(g1), full(w), full(lng), full(lnb), full(ws), full(bs), full(wdec), full(bdec)],
        out_specs=[vt_spec if o is None else row(o[0]) for o in outs] + [st_spec],
        out_shape=[vt_shape if o is None else jax.ShapeDtypeStruct((T, o[0]), o[1]) for o in outs] + [st_shape],
        compiler_params=_cparams(("parallel",)),
        name="inproj",
    )(x2d, g1, w, lng, lnb, ws, bs, wdec, bdec)


N_POS_LANES = 9


def _split3(x):
    hi = x.astype(BF16)
    r1 = x - hi.astype(F32)
    mid = r1.astype(BF16)
    lo = (r1 - mid.astype(F32)).astype(BF16)
    return hi, mid, lo


def _attn_kernel(lq1_ref, lk1_ref, lq2_ref, lk2_ref, g_ref, kpos_ref, st_ref, q_ref, k_ref, vt_ref, o_ref,
                 qall_ref, *chain_refs, tq, tk, nq, lam_init):
    nch = 2 * nq
    grp = lambda g: chain_refs[g * nch:(g + 1) * nch]
    m_refs, l_refs, acc_refs = grp(0), grp(1), grp(2)
    s_refs, p_refs, al_refs = (grp(3), grp(4)), (grp(5), grp(6)), (grp(7), grp(8))
    h = pl.program_id(1)
    qi = pl.program_id(2)
    nblk = k_ref.shape[1] // tk
    nrest = nblk - 1
    lam = (jnp.exp(jnp.sum(lq1_ref[...] * lk1_ref[...], axis=-1, keepdims=True))
           - jnp.exp(jnp.sum(lq2_ref[...] * lk2_ref[...], axis=-1, keepdims=True)) + lam_init)
    c = jnp.exp2(-8.0 * (h + 1).astype(F32) / A_HEADS) * LOG2E

    lane = lax.broadcasted_iota(jnp.int32, (tq, 128), 1)
    il = lax.broadcasted_iota(jnp.int32, (tq, 128), 0).astype(F32)
    hi, mid, lo = _split3(jnp.where(lane < 3, -c * il, c))
    piece = lane % 3
    qpos = jnp.where(piece == 0, hi, jnp.where(piece == 1, mid, lo))
    qpos = jnp.where(lane < N_POS_LANES, qpos, jnp.zeros_like(qpos))
    chains = [(t, m) for t in range(nq) for m in range(2)]
    for ci, (t, m) in enumerate(chains):
        q = q_ref[0, t * tq:(t + 1) * tq, :]
        keep = (lane < A_QK_DIM) if m == 0 else (lane >= A_QK_DIM)
        qm = jnp.where(keep, q, jnp.zeros_like(q))
        qall_ref[0, ci] = jnp.concatenate([qm, qpos], axis=1)
        qall_ref[1, ci] = jnp.concatenate([qm, -qpos], axis=1)
    kpos = kpos_ref[...]

    def block_of(i):
        above = (i >= qi).astype(jnp.int32)
        return i + above, above

    def scores(j, variant, ci):
        kaug = jnp.concatenate([k_ref[0, pl.ds(pl.multiple_of(j * tk, tk), tk), :], kpos], axis=1)
        return lax.dot_general(kaug, qall_ref[variant, ci], (((1,), (1,)), ((), ())),
                               preferred_element_type=F32)

    def softmax(j, ci, s, first):
        t, _ = chains[ci]
        off = ((qi * nq + t) * tq - j * tk).astype(F32)
        delta = c * jnp.abs(off)
        if first:
            rel = (lax.broadcasted_iota(jnp.int32, (tk, tq), 0)
                   - lax.broadcasted_iota(jnp.int32, (tk, tq), 1)).astype(F32)
            s = s - (2.0 * c) * jnp.maximum(rel - float(t * tq), 0.0)
        bmax = jnp.max(s, axis=0, keepdims=True) - delta
        if first:
            mn = bmax
        else:
            mx = m_refs[ci][...]
            mn = jnp.maximum(mx, bmax)
        p = jnp.exp2(s - (mn + delta))
        psum = jnp.sum(p, axis=0, keepdims=True)
        m_refs[ci][...] = mn
        if first:
            l_refs[ci][...] = psum
            return p.astype(BF16), None
        alpha = jnp.exp2(mx - mn)
        l_refs[ci][...] = alpha * l_refs[ci][...] + psum
        return p.astype(BF16), alpha

    n = nblk
    if n >= 4:
        st = st_ref[...]
        mine = st_ref[qi]
        jv = lax.broadcasted_iota(jnp.int32, (n, 128), 0)
        gl = lax.broadcasted_iota(jnp.int32, (n, 128), 1)
        ub = jnp.full((n, 128), -jnp.inf, F32)
        for t in range(nq):
            i0 = (qi * nq + t) * tq
            gap = jnp.where(jv < qi, i0 - (jv * tk + tk - 1), jv * tk - (i0 + tq - 1))
            dmin = jnp.maximum(gap, 0).astype(F32)
            bound = 1.1 * jnp.sqrt(mine[1 + t:2 + t, :] * st[:, 0, :]) - mine[1 + nq + t:2 + nq + t, :] - c * dmin
            ub = jnp.maximum(ub, bound)
        ub = jnp.where(gl // 2 == h, ub, -jnp.inf)
        live = jnp.logical_or(jnp.max(ub, axis=1, keepdims=True) > -150.0, jv[:, :1] == qi)
        j1 = jv[:, :1]
        jlo = jnp.min(jnp.where(live, j1, n))
        jhi = jnp.max(jnp.where(live, j1, -1))
        cnt = jhi - jlo
        need = jnp.maximum(cnt + 1 - cnt % 2, 3)
        grow_hi = jnp.minimum(need - cnt, (n - 1) - jhi)
        lo = jlo - (need - cnt - grow_hi)
        nvis = need + 1
    else:
        lo = 0
        nvis = n

    def visit(u):
        if isinstance(u, int) and u == 0:
            return qi, 0
        return block_of(lo + u - 1)

    def tick(tau, do_qk, do_sm, do_pv):
        tau, par = tau
        first_sm = isinstance(tau, int) and tau == 1
        first_pv = isinstance(tau, int) and tau == 2
        if do_qk:
            jq, vq = visit(tau)
        if do_sm:
            js, _ = visit(tau - 1)
        if do_pv:
            jp, _ = visit(tau - 2)
        for half in range(0, nch, 2):
            for ci in (half, half + 1):
                if do_qk:
                    s_refs[par][ci][...] = scores(jq, vq, ci)
            for ci in (half, half + 1):
                if do_sm:
                    p, alpha = softmax(js, ci, s_refs[1 - par][ci][...], first_sm)
                    p_refs[1 - par][ci][...] = p
                    if not first_sm:
                        al_refs[1 - par][ci][...] = alpha
            for ci in (half, half + 1):
                if do_pv:
                    pv = jnp.dot(vt_ref[jp], p_refs[par][ci][...], preferred_element_type=F32)
                    if first_pv:
                        acc_refs[ci][...] = pv
                    else:
                        acc_refs[ci][...] = al_refs[par][ci][...] * acc_refs[ci][...] + pv

    assert n == 1 or n % 2 == 0
    tick((0, 0), True, False, False)
    tick((1, 1), n > 1, True, False)
    if n == 1:
        tick((2, 0), False, False, True)
    else:
        tick((2, 0), n > 2, True, True)
        if n > 2:
            tick((3, 1), True, True, True)

            def pair(ip, carry):
                tau = 4 + 2 * ip
                tick((tau, 0), True, True, True)
                tick((tau + 1, 1), True, True, True)
                return carry

            lax.fori_loop(0, (nvis - 4) // 2, pair, 0)
            tick((nvis, 0), False, True, True)
        tick((nvis + 1, 1), False, False, True)

    for t in range(nq):
        a0, a1 = acc_refs[2 * t][...], acc_refs[2 * t + 1][...]
        o = a0 / l_refs[2 * t][...] - lam * (a1 / l_refs[2 * t + 1][...])
        ms = jnp.mean(o * o, axis=0, keepdims=True)
        o = o * lax.rsqrt(ms + EPS) * g_ref[...] * (1.0 - lam_init)
        o_ref[0, t * tq:(t + 1) * tq, :] = jnp.transpose(o).astype(o_ref.dtype)


def _attention(aq, ak, avt, stats, lq1, lk1, lq2, lk2, subln_col, kpos, lam_init, tq, tk):
    B, S, _ = aq.shape
    nq = tk // tq
    nch = 2 * nq
    assert S == tk or (S // tk) % 2 == 0
    vec = pl.BlockSpec((1, A_QK_DIM), lambda b, h, i: (0, 0))
    kern = functools.partial(_attn_kernel, tq=tq, tk=tk, nq=nq, lam_init=lam_init)
    return pl.pallas_call(
        kern,
        grid=(B, A_HEADS, S // tk),
        in_specs=[vec, vec, vec, vec,
                  pl.BlockSpec((A_V_DIM, 1), lambda b, h, i: (0, 0)),
                  pl.BlockSpec((tk, 128), lambda b, h, i: (0, 0)),
                  pl.BlockSpec((S // tk, 8, 128), lambda b, h, i: (b, 0, 0)),
                  pl.BlockSpec((1, tk, 128), lambda b, h, i: (b, i, h)),
                  pl.BlockSpec((1, S, 128), lambda b, h, i: (b, 0, h)),
                  pl.BlockSpec((S // tk, A_V_DIM, tk), lambda b, h, i: (b, h, 0))],
        out_specs=pl.BlockSpec((1, tk, 128), lambda b, h, i: (b, i, h)),
        out_shape=jax.ShapeDtypeStruct((B, S, A_WIDTH), BF16),
        scratch_shapes=([pltpu.VMEM((2, nch, tq, 256), BF16)]
                        + [pltpu.VMEM((1, tq), F32)] * (2 * nch)
                        + [pltpu.VMEM((A_V_DIM, tq), F32)] * nch
                        + [pltpu.VMEM((tk, tq), F32)] * (2 * nch)
                        + [pltpu.VMEM((tk, tq), BF16)] * (2 * nch)
                        + [pltpu.VMEM((1, tq), F32)] * (2 * nch)),
        compiler_params=_cparams(("parallel", "parallel", "arbitrary")),
        name="diff_attn",
    )(lq1, lk1, lq2, lk2, subln_col, kpos, stats, aq, ak, avt)


def _key_pos_lanes(tk):
    j = jnp.arange(tk, dtype=jnp.int32)
    jlo = (j % 256).astype(F32)
    jhi = (j - j % 256).astype(F32)
    cols = [jnp.ones((tk,), F32)] * 3 + [jlo] * 3 + [jhi] * 3
    kp = jnp.stack(cols, axis=1)
    return jnp.pad(kp, ((0, 0), (0, 128 - N_POS_LANES))).astype(BF16)


def _gla_chunk(q, k, v, la, st, reverse):
    C = C_CHUNK
    r = lax.broadcasted_iota(jnp.int32, (C, C), 0)
    c = lax.broadcasted_iota(jnp.int32, (C, C), 1)
    tri = (c >= r) if reverse else (c <= r)
    tri_b = tri.astype(BF16)
    la_hi = la.astype(BF16)
    la_lo = (la - la_hi.astype(F32)).astype(BF16)
    b = (jnp.dot(tri_b, la_hi, preferred_element_type=F32)
         + jnp.dot(tri_b, la_lo, preferred_element_type=F32))
    b_end = b[0:1, :] if reverse else b[C - 1:C, :]
    eb = jnp.exp(b)
    qt = (q * eb).astype(BF16)
    kt = k * jnp.exp(-b)
    kdec = (k * jnp.exp(b_end - b)).astype(BF16)

    lane_k = lax.broadcasted_iota(jnp.int32, (C, C_KEY_WIDTH), 1) // C_K_DIM
    kstack = jnp.concatenate([jnp.where(lane_k == hh, kt, 0.0) for hh in range(C_HEADS)], axis=0).astype(BF16)
    attn = lax.dot_general(qt, kstack, (((1,), (1,)), ((), ())), preferred_element_type=F32)
    ri = lax.broadcasted_iota(jnp.int32, (C, C_HEADS * C), 0)
    cj = lax.broadcasted_iota(jnp.int32, (C, C_HEADS * C), 1) % C
    keep = (cj >= ri) if reverse else (cj <= ri)
    attn = jnp.where(keep, attn, 0.0).astype(BF16)
    lane_v = lax.broadcasted_iota(jnp.int32, (C, C_WIDTH), 1) // C_V_DIM
    vstack = jnp.concatenate([jnp.where(lane_v == hh, v, jnp.zeros_like(v)) for hh in range(C_HEADS)], axis=0)
    o = jnp.dot(attn, vstack, preferred_element_type=F32)
    o = o + lax.dot_general(qt, st.astype(BF16), (((1,), (1,)), ((), ())), preferred_element_type=F32)

    upd = lax.dot_general(v, kdec, (((0,), (0,)), ((), ())), preferred_element_type=F32)
    rh = lax.broadcasted_iota(jnp.int32, (C_WIDTH, C_KEY_WIDTH), 0) // C_V_DIM
    ch = lax.broadcasted_iota(jnp.int32, (C_WIDTH, C_KEY_WIDTH), 1) // C_K_DIM
    st = st * jnp.exp(b_end) + jnp.where(rh == ch, upd, 0.0)
    return o, st


def _gla_fwd_kernel(q_ref, k_ref, v_ref, la_ref, o_ref, st_ref, *, nchunk):
    @pl.when(pl.program_id(1) == 0)
    def _():
        st_ref[...] = jnp.zeros_like(st_ref)

    G = q_ref.shape[0]
    st = [st_ref[g] for g in range(G)]
    for c in range(nchunk):
        rows = slice(c * C_CHUNK, (c + 1) * C_CHUNK)
        for g in range(G):
            o, st[g] = _gla_chunk(q_ref[g, rows, :], k_ref[g, rows, :], v_ref[g, rows, :], la_ref[g, rows, :],
                                  st[g], False)
            o_ref[g, rows, :] = o
    for g in range(G):
        st_ref[g] = st[g]


def _gla_bwd_kernel(q_ref, k_ref, v_ref, la_ref, of_ref, g_ref, gn_ref, o_ref, st_ref, *, nchunk):
    @pl.when(pl.program_id(1) == 0)
    def _():
        st_ref[...] = jnp.zeros_like(st_ref)

    G = q_ref.shape[0]
    st = [st_ref[g] for g in range(G)]
    lane_h = lax.broadcasted_iota(jnp.int32, (C_WIDTH, C_WIDTH), 0) // C_V_DIM
    lane_h2 = lax.broadcasted_iota(jnp.int32, (C_WIDTH, C_WIDTH), 1) // C_V_DIM
    segb = ((lane_h == lane_h2).astype(F32) * (1.0 / C_V_DIM)).astype(BF16)
    for c in reversed(range(nchunk)):
        rows = slice(c * C_CHUNK, (c + 1) * C_CHUNK)
        for g in range(G):
            o, st[g] = _gla_chunk(q_ref[g, rows, :], k_ref[g, rows, :], v_ref[g, rows, :], la_ref[g, rows, :],
                                  st[g], True)
            o = o + of_ref[g, rows, :]
            sq = o * o
            sq_hi = sq.astype(BF16)
            sq_lo = (sq - sq_hi.astype(F32)).astype(BF16)
            ms = (jnp.dot(sq_hi, segb, preferred_element_type=F32)
                  + jnp.dot(sq_lo, segb, preferred_element_type=F32))
            o = o * lax.rsqrt(ms + EPS) * gn_ref[...]
            gate = g_ref[g, rows, :]
            o_ref[g, rows, :] = (o * (gate * jax.nn.sigmoid(gate))).astype(o_ref.dtype)
    for g in range(G):
        st_ref[g] = st[g]


def _gla(cq, ck, cv, cg, laf, lab, gn_tiled, rb):
    B, S, _ = cq.shape
    nb = S // rb
    nchunk = rb // C_CHUNK
    G = 2 if B % 2 == 0 else 1
    fspec = lambda n: pl.BlockSpec((G, rb, n), lambda b, i: (b, i, 0))
    bspec = lambda n: pl.BlockSpec((G, rb, n), lambda b, i: (b, nb - 1 - i, 0))
    st = pltpu.VMEM((G, C_WIDTH, C_KEY_WIDTH), F32)
    o_f = pl.pallas_call(
        functools.partial(_gla_fwd_kernel, nchunk=nchunk),
        grid=(B // G, nb),
        in_specs=[fspec(C_KEY_WIDTH), fspec(C_KEY_WIDTH), fspec(C_WIDTH), fspec(C_KEY_WIDTH)],
        out_specs=fspec(C_WIDTH),
        out_shape=jax.ShapeDtypeStruct((B, S, C_WIDTH), F32),
        scratch_shapes=[st],
        compiler_params=_cparams(("parallel", "arbitrary")),
        name="gla_fwd",
    )(cq, ck, cv, laf)
    return pl.pallas_call(
        functools.partial(_gla_bwd_kernel, nchunk=nchunk),
        grid=(B // G, nb),
        in_specs=[bspec(C_KEY_WIDTH), bspec(C_KEY_WIDTH), bspec(C_WIDTH), bspec(C_KEY_WIDTH),
                  bspec(C_WIDTH), bspec(C_WIDTH), pl.BlockSpec((1, C_WIDTH), lambda b, i: (0, 0))],
        out_specs=bspec(C_WIDTH),
        out_shape=jax.ShapeDtypeStruct((B, S, C_WIDTH), BF16),
        scratch_shapes=[st],
        compiler_params=_cparams(("parallel", "arbitrary")),
        name="gla_bwd",
    )(cq, ck, cv, lab, o_f, cg, gn_tiled)


def _outproj_kernel(x_ref, oa_ref, ob_ref, oc_ref, w_ref, g2_ref, wr_hl_ref, wr_hi_ref,
                    h_ref, hn_ref, aff_ref, *, sub):
    for r in range(x_ref.shape[0] // sub):
        rows = slice(r * sub, (r + 1) * sub)
        mix = jnp.concatenate([oa_ref[rows, :], ob_ref[rows, :], oc_ref[rows, :]], axis=1)
        h = x_ref[rows, :] + jnp.dot(mix, w_ref[...], preferred_element_type=F32)
        h_ref[rows, :] = h
        ms = jnp.mean(h * h, axis=-1, keepdims=True)
        hn = h * lax.rsqrt(ms + EPS) * g2_ref[...]
        hn_ref[rows, :] = _pack_bf16_pairs(hn)
        hn_hi = hn.astype(BF16)
        hn_lo = (hn - hn_hi.astype(F32)).astype(BF16)
        both = jnp.dot(hn_hi, wr_hl_ref[...], preferred_element_type=F32)
        logits = both[:, :128] + both[:, 128:] + jnp.dot(hn_lo, wr_hi_ref[...], preferred_element_type=F32)
        lane = lax.broadcasted_iota(jnp.int32, logits.shape, 1)
        logits = jnp.where(lane < N_EXPERTS, logits, -jnp.inf)
        mx = jnp.max(logits, axis=-1, keepdims=True)
        e = jnp.exp(logits - mx)
        aff = e / jnp.sum(e, axis=-1, keepdims=True)
        aff_ref[:, rows] = jnp.transpose(aff)[:N_EXPERTS, :]


def _outproj(x2d, oa, ob, oc, w, g2, wr_hi, wr_lo, tm):
    T, D = x2d.shape
    wr_hl = jnp.concatenate([wr_hi, wr_lo], axis=1)
    row = lambda n: pl.BlockSpec((tm, n), lambda i: (i, 0))
    full = lambda a: pl.BlockSpec(a.shape, lambda i: (0,) * a.ndim)
    return pl.pallas_call(
        functools.partial(_outproj_kernel, sub=min(tm, 256)),
        grid=(T // tm,),
        in_specs=[row(D), row(A_WIDTH), row(B_WIDTH), row(C_WIDTH), full(w), full(g2), full(wr_hl), full(wr_hi)],
        out_specs=[row(D), row(D // 2), pl.BlockSpec((N_EXPERTS, tm), lambda i: (0, i))],
        out_shape=[jax.ShapeDtypeStruct((T, D), F32), jax.ShapeDtypeStruct((T, D // 2), jnp.uint32),
                   jax.ShapeDtypeStruct((N_EXPERTS, T), F32)],
        compiler_params=_cparams(("parallel",)),
        name="outproj_router",
    )(x2d, oa, ob, oc, w, g2, wr_hl, wr_hi)


def _cumsum_tokens(x, upper, lstrict, ones):
    G, R, _ = x.shape
    xb = x.reshape(G * R, 128).astype(BF16)
    within = jnp.dot(xb, upper, preferred_element_type=F32).reshape(G, R, 128)
    rowtot = jnp.dot(xb, ones, preferred_element_type=F32).reshape(G, R, 128)
    hi = jnp.floor(rowtot * (1.0 / 256.0))
    lo = rowtot - 256.0 * hi
    outs = []
    for g in range(G):
        before = (256.0 * jnp.dot(lstrict, hi[g].astype(BF16), preferred_element_type=F32)
                  + jnp.dot(lstrict, lo[g].astype(BF16), preferred_element_type=F32))
        outs.append(within[g] + before)
    return jnp.stack(outs, axis=0)


def _route_kernel(aff_ref, sel_ref, pos_ref, dst_ref, off_ref, cnt_ref, *, cap):
    E, R, _ = aff_ref.shape
    bits = pltpu.bitcast(aff_ref[...].reshape(E * R, 128), jnp.int32).reshape(E, R, 128)

    def count_ge(v):
        return jnp.sum(jnp.sum((bits >= v).astype(jnp.int32), axis=1, keepdims=True), axis=2, keepdims=True)

    def bisect(i, prefix):
        cand = prefix | jnp.left_shift(jnp.int32(1), 30 - i)
        return jnp.where(count_ge(cand) >= cap, cand, prefix)

    thr = lax.fori_loop(0, 31, bisect, jnp.zeros((E, 1, 1), jnp.int32))
    gt = bits > thr
    eq = bits == thr
    n_gt = jnp.sum(jnp.sum(gt.astype(jnp.int32), axis=1, keepdims=True), axis=2, keepdims=True)
    need_eq = (cap - n_gt).astype(F32)

    r = lax.broadcasted_iota(jnp.int32, (128, 128), 0)
    cc = lax.broadcasted_iota(jnp.int32, (128, 128), 1)
    upper = (r <= cc).astype(BF16)
    ones = jnp.ones((128, 128), BF16)
    rr = lax.broadcasted_iota(jnp.int32, (R, R), 0)
    rc = lax.broadcasted_iota(jnp.int32, (R, R), 1)
    lstrict = (rc < rr).astype(BF16)

    eqf = eq.astype(F32)
    eq_rank = _cumsum_tokens(eqf, upper, lstrict, ones) - eqf
    sel = jnp.logical_or(gt, jnp.logical_and(eq, eq_rank < need_eq))
    self_ = sel.astype(F32)
    pos = _cumsum_tokens(self_, upper, lstrict, ones) - self_
    cnt = jnp.sum(self_, axis=0, keepdims=True)
    off = _cumsum_tokens(cnt, upper, lstrict, ones) - cnt
    rank = jnp.zeros((R, 128), F32)
    for e in range(E):
        dst_ref[e] = (off[0] + rank).astype(jnp.int32)
        rank = rank + self_[e]
    sel_ref[...] = sel.astype(jnp.int32)
    pos_ref[...] = pos.astype(jnp.int32)
    off_ref[...] = off[0].astype(jnp.int32)
    cnt_ref[...] = cnt[0].astype(jnp.int32)


def _route(aff_t, cap):
    E, T = aff_t.shape
    R = T // 128
    a3 = aff_t.reshape(E, R, 128)
    i3 = jax.ShapeDtypeStruct((E, R, 128), jnp.int32)
    i2 = jax.ShapeDtypeStruct((R, 128), jnp.int32)
    return pl.pallas_call(
        functools.partial(_route_kernel, cap=cap),
        out_shape=[i3, i3, i3, i2, i2],
        compiler_params=pltpu.CompilerParams(vmem_limit_bytes=VMEM_LIMIT),
        name="route_select",
    )(a3)


SC_CHUNK = 2048
SC_ROWS = 64


def _sc_mesh():
    return plsc.VectorSubcoreMesh(core_axis_name="c", subcore_axis_name="s")


def _sc_worker():
    info = pltpu.get_tpu_info().sparse_core
    return lax.axis_index("s") * info.num_cores + lax.axis_index("c"), info.num_cores * info.num_subcores


def _sc_compact(sel, pos, dst, aff, E, T, cap):
    @functools.partial(
        pl.kernel, mesh=_sc_mesh(),
        out_type=[jax.ShapeDtypeStruct((E * cap,), jnp.int32), jax.ShapeDtypeStruct((E * cap,), F32),
                  jax.ShapeDtypeStruct((E * cap,), jnp.int32)],
        scratch_types=[pltpu.VMEM((SC_CHUNK,), jnp.int32), pltpu.VMEM((SC_CHUNK,), jnp.int32),
                       pltpu.VMEM((SC_CHUNK,), jnp.int32), pltpu.VMEM((SC_CHUNK,), F32),
                       pltpu.VMEM((cap,), jnp.int32), pltpu.VMEM((cap,), F32), pltpu.VMEM((cap,), jnp.int32)],
        compiler_params=pltpu.CompilerParams(needs_layout_passes=False),
    )
    def k(sel_hbm, pos_hbm, dst_hbm, aff_hbm, idx_hbm, gate_hbm, dstl_hbm,
          sel_v, pos_v, dst_v, aff_v, idx_b, gate_b, dstl_b):
        e, _ = _sc_worker()

        @pl.when(e < E)
        def _():
            @pl.loop(0, T // SC_CHUNK)
            def _(ch):
                base = pl.multiple_of(e * T + ch * SC_CHUNK, 8)
                pltpu.sync_copy(sel_hbm.at[pl.ds(base, SC_CHUNK)], sel_v)
                pltpu.sync_copy(pos_hbm.at[pl.ds(base, SC_CHUNK)], pos_v)
                pltpu.sync_copy(dst_hbm.at[pl.ds(base, SC_CHUNK)], dst_v)
                pltpu.sync_copy(aff_hbm.at[pl.ds(base, SC_CHUNK)], aff_v)

                @pl.loop(0, SC_CHUNK // 16)
                def _(i):
                    sl = pl.ds(pl.multiple_of(i * 16, 16), 16)
                    chosen = sel_v[sl] > 0
                    slot = pos_v[sl]
                    tok = ch * SC_CHUNK + i * 16 + lax.iota(jnp.int32, 16)
                    plsc.store_scatter(idx_b, [slot], tok, mask=chosen)
                    plsc.store_scatter(gate_b, [slot], aff_v[sl], mask=chosen)
                    plsc.store_scatter(dstl_b, [slot], dst_v[sl], mask=chosen)

            out = pl.ds(pl.multiple_of(e * cap, 8), cap)
            pltpu.sync_copy(idx_b, idx_hbm.at[out])
            pltpu.sync_copy(gate_b, gate_hbm.at[out])
            pltpu.sync_copy(dstl_b, dstl_hbm.at[out])

    return k(sel, pos, dst, aff)


def _sc_gather_rows(table, idx):
    B, (_, D) = idx.shape[0], table.shape

    @functools.partial(
        pl.kernel, mesh=_sc_mesh(), out_type=jax.ShapeDtypeStruct((B, D), table.dtype),
        scratch_types=[pltpu.VMEM((SC_ROWS,), jnp.int32), pltpu.VMEM((SC_ROWS, D), table.dtype),
                       pltpu.SemaphoreType.DMA])
    def k(table_hbm, idx_hbm, out_hbm, idx_v, rows_v, sem):
        wid, nw = _sc_worker()
        per = B // nw

        @pl.loop(0, per // SC_ROWS)
        def _(g):
            off = pl.multiple_of(wid * per + g * SC_ROWS, 8)
            pltpu.sync_copy(idx_hbm.at[pl.ds(off, SC_ROWS)], idx_v)
            pltpu.async_copy(table_hbm.at[idx_v], rows_v, sem).wait()
            pltpu.sync_copy(rows_v, out_hbm.at[pl.ds(off, SC_ROWS)])

    return k(table, idx)


def _sc_scatter_rows(rows, dst):
    B, D = rows.shape

    @functools.partial(
        pl.kernel, mesh=_sc_mesh(), out_type=jax.ShapeDtypeStruct((B, D), rows.dtype),
        scratch_types=[pltpu.VMEM((SC_ROWS,), jnp.int32), pltpu.VMEM((SC_ROWS, D), rows.dtype),
                       pltpu.SemaphoreType.DMA])
    def k(rows_hbm, dst_hbm, out_hbm, dst_v, rows_v, sem):
        wid, nw = _sc_worker()
        per = B // nw

        @pl.loop(0, per // SC_ROWS)
        def _(g):
            off = pl.multiple_of(wid * per + g * SC_ROWS, 8)
            pltpu.sync_copy(dst_hbm.at[pl.ds(off, SC_ROWS)], dst_v)
            pltpu.sync_copy(rows_hbm.at[pl.ds(off, SC_ROWS)], rows_v)
            pltpu.async_copy(rows_v, out_hbm.at[dst_v], sem).wait()

    return k(rows, dst)


def _pack_bf16_pairs(x):
    m = x.shape[1] // 2
    bits = pltpu.bitcast(x.astype(BF16).astype(F32), jnp.uint32)
    return (bits[:, m:] & jnp.uint32(0xFFFF0000)) | (bits[:, :m] >> 16)


def _unpack_bf16_pairs(u):
    lo = pltpu.bitcast(u << 16, F32)
    hi = pltpu.bitcast(u & jnp.uint32(0xFFFF0000), F32)
    return jnp.concatenate([lo, hi], axis=1).astype(BF16)


def _combine_kernel(start_ref, h_ref, off_ref, cnt_ref, gf_ref, y_hbm, o_ref, ybuf, sem, acc_ref, nwin_done,
                    *, tb, win, nrows, final_norm):
    b = pl.program_id(0)
    nb = pl.num_programs(0)

    def window_start(blk, w):
        a8 = (start_ref[blk] // 8) * 8
        return pl.multiple_of(jnp.minimum(a8 + w * win, nrows - win), 8)

    def n_windows(blk):
        a8 = (start_ref[blk] // 8) * 8
        return jnp.maximum((start_ref[blk + 1] - a8 + win - 1) // win, 1)

    def copy(blk, w, slot):
        return pltpu.make_async_copy(y_hbm.at[pl.ds(window_start(blk, w), win)], ybuf.at[slot], sem.at[slot])

    @pl.when(b == 0)
    def _():
        nwin_done[0] = 0
        copy(0, 0, 0).start()

    rows = off_ref.shape[1]
    off_row = jnp.concatenate([off_ref[0, i:i + 1, :] for i in range(rows)], axis=1)
    end_row = off_row + jnp.concatenate([cnt_ref[0, i:i + 1, :] for i in range(rows)], axis=1)
    acc_ref[...] = jnp.zeros_like(acc_ref)
    nwin = n_windows(b)

    def body(w, carry):
        g = nwin_done[0]
        slot = g % 2
        copy(b, w, slot).wait()

        @pl.when(w + 1 < nwin)
        def _():
            copy(b, w + 1, 1 - slot).start()

        @pl.when(jnp.logical_and(w + 1 == nwin, b + 1 < nb))
        def _():
            copy(b + 1, 0, 1 - slot).start()

        first_new = (start_ref[b] // 8) * 8 + w * win
        r = window_start(b, w) + lax.broadcasted_iota(jnp.int32, (win, tb), 0)
        onehot_t = jnp.logical_and(jnp.logical_and(r >= off_row, r < end_row), r >= first_new)
        onehot_t = jnp.where(onehot_t, 1.0, 0.0).astype(BF16)
        y = _unpack_bf16_pairs(ybuf[slot])
        acc_ref[...] += lax.dot_general(onehot_t, y, (((0,), (0,)), ((), ())), preferred_element_type=F32)
        nwin_done[0] = g + 1
        return carry

    lax.fori_loop(0, nwin, body, 0)
    o = h_ref[...] + acc_ref[...]
    if final_norm:
        ms = jnp.mean(o * o, axis=-1, keepdims=True)
        o = o * lax.rsqrt(ms + EPS) * gf_ref[...]
    o_ref[...] = o


def _combine(h, off2d, cnt2d, blk_start, y_sorted, gf, tb, win, final_norm):
    T, D = h.shape
    nrows = y_sorted.shape[0]
    win = min(win, nrows)
    off3 = off2d.reshape(T // tb, tb // 128, 128)
    cnt3 = cnt2d.reshape(T // tb, tb // 128, 128)
    kern = functools.partial(_combine_kernel, tb=tb, win=win, nrows=nrows, final_norm=final_norm)
    grid_spec = pltpu.PrefetchScalarGridSpec(
        num_scalar_prefetch=1,
        grid=(T // tb,),
        in_specs=[pl.BlockSpec((tb, D), lambda i, s: (i, 0)),
                  pl.BlockSpec((1, tb // 128, 128), lambda i, s: (i, 0, 0)),
                  pl.BlockSpec((1, tb // 128, 128), lambda i, s: (i, 0, 0)),
                  pl.BlockSpec((1, D), lambda i, s: (0, 0)),
                  pl.BlockSpec(memory_space=pl.ANY)],
        out_specs=pl.BlockSpec((tb, D), lambda i, s: (i, 0)),
        scratch_shapes=[pltpu.VMEM((2, win, D // 2), jnp.uint32), pltpu.SemaphoreType.DMA((2,)),
                        pltpu.VMEM((tb, D), F32), pltpu.SMEM((1,), jnp.int32)],
    )
    return pl.pallas_call(
        kern, grid_spec=grid_spec, out_shape=jax.ShapeDtypeStruct((T, D), F32),
        compiler_params=_cparams(("arbitrary",)),
        name="combine",
    )(blk_start, h, off3, cnt3, gf, y_sorted)


def _ffn_kernel(x_ref, gate_ref, wg_ref, wu_ref, wd_ref, o_ref, *, fc):
    x = _unpack_bf16_pairs(x_ref[0])
    F = wg_ref.shape[2]
    acc = jnp.zeros((x.shape[0], wd_ref.shape[2]), F32)
    for c in range(F // fc):
        cols = slice(c * fc, (c + 1) * fc)
        g = jnp.dot(x, wg_ref[0, :, cols], preferred_element_type=F32)
        u = jnp.dot(x, wu_ref[0, :, cols], preferred_element_type=F32)
        hid = (g * jax.nn.sigmoid(g) * u).astype(BF16)
        acc = acc + jnp.dot(hid, wd_ref[0, cols, :], preferred_element_type=F32)
    o_ref[0] = _pack_bf16_pairs(acc * gate_ref[0])


def _ffn(xe, gate, wg, wu, wd, tm, fc):
    E, cap, Dh = xe.shape
    D = 2 * Dh
    F = wg.shape[2]
    fc = min(fc, F)
    return pl.pallas_call(
        functools.partial(_ffn_kernel, fc=fc),
        grid=(E, cap // tm),
        in_specs=[pl.BlockSpec((1, tm, Dh), lambda e, i: (e, i, 0)),
                  pl.BlockSpec((1, tm, 1), lambda e, i: (e, i, 0)),
                  pl.BlockSpec((1, D, F), lambda e, i: (e, 0, 0)),
                  pl.BlockSpec((1, D, F), lambda e, i: (e, 0, 0)),
                  pl.BlockSpec((1, F, D), lambda e, i: (e, 0, 0))],
        out_specs=pl.BlockSpec((1, tm, Dh), lambda e, i: (e, i, 0)),
        out_shape=jax.ShapeDtypeStruct((E, cap, Dh), jnp.uint32),
        compiler_params=_cparams(("parallel", "arbitrary")),
        name="expert_ffn",
    )(xe, gate, wg, wu, wd)


def _prep_layer(l, norm1, w_in, ln_v_g, ln_v_b, w_spatial, b_spatial, gla_decay_w, gla_decay_b, gla_norm,
                w_out, norm2, w_router, diff_subln):
    D = w_in.shape[1]
    w = jnp.pad(w_in[l], ((0, 0), (0, PROJ_PAD - PROJ_WIDTH))).astype(BF16)
    bs = jnp.repeat(b_spatial[l].T, B_GROUP_DIM, axis=1)
    wdec = jnp.zeros((128, 2 * C_KEY_WIDTH), F32)
    wdec = wdec.at[:C_DECAY_RANK, :C_KEY_WIDTH].set(gla_decay_w[l, 0])
    wdec = wdec.at[C_DECAY_RANK:2 * C_DECAY_RANK, C_KEY_WIDTH:].set(gla_decay_w[l, 1])
    bdec = gla_decay_b[l].reshape(1, 2 * C_KEY_WIDTH)
    wr = jnp.pad(w_router[l], ((0, 0), (0, 128 - N_EXPERTS)))
    wr_hi = wr.astype(BF16)
    wr_lo = (wr - wr_hi.astype(F32)).astype(BF16)
    return dict(
        g1=norm1[l].reshape(1, D), w=w, lng=ln_v_g[l].reshape(1, B_WIDTH), lnb=ln_v_b[l].reshape(1, B_WIDTH),
        ws=w_spatial[l].astype(BF16), bs=bs, wdec=wdec.astype(BF16), bdec=bdec,
        gn=jnp.tile(gla_norm[l], C_HEADS).reshape(1, C_WIDTH), w_out=w_out[l].astype(BF16),
        g2=norm2[l].reshape(1, D), wr_hi=wr_hi, wr_lo=wr_lo, subln=diff_subln[l].reshape(A_V_DIM, 1),
    )


def _tile(n, pref):
    t = min(n, pref)
    while n % t:
        t //= 2
    return t


def _trunk(x, layers, lam_vecs, ffn_w, norm_f):
    B, S, D = x.shape
    T = B * S
    cap = EC_CAPACITY_FACTOR * T // N_EXPERTS
    tm = _tile(T, 512)
    tk = _tile(S, 512)
    tq = min(tk, 256)
    kpos = _key_pos_lanes(tk)
    x2d = x.reshape(T, D)
    for l, p in enumerate(layers):
        lam_init = 0.8 - 0.6 * math.exp(-0.3 * l)
        aq, ak, avt, ob, cq, ck, cv, cg, laf, lab, stats = _inproj(
            x2d, p["g1"], p["w"], p["lng"], p["lnb"], p["ws"], p["bs"], p["wdec"], p["bdec"], tm, min(tk, tm), tq)
        r3 = lambda a: a.reshape(B, S, a.shape[-1])
        lq1, lk1, lq2, lk2 = (v[l:l + 1] for v in lam_vecs)
        oa = _attention(r3(aq), r3(ak), avt, stats, lq1, lk1, lq2, lk2, p["subln"], kpos, lam_init, tq, tk)
        oc = _gla(r3(cq), r3(ck), r3(cv), r3(cg), r3(laf), r3(lab), p["gn"], _tile(S, 512))
        h, hn, aff_t = _outproj(x2d, oa.reshape(T, A_WIDTH), ob, oc.reshape(T, C_WIDTH),
                                p["w_out"], p["g2"], p["wr_hi"], p["wr_lo"], tm)
        sel, pos, dst, off2d, cnt2d = _route(aff_t, cap)
        flat = lambda a: a.reshape(N_EXPERTS * T)
        idx, gate, dstl = _sc_compact(flat(sel), flat(pos), flat(dst), flat(aff_t), N_EXPERTS, T, cap)
        xe = _sc_gather_rows(hn, idx)
        wg, wu, wd = (w[l] for w in ffn_w)
        ye = _ffn(xe.reshape(N_EXPERTS, cap, D // 2), gate.reshape(N_EXPERTS, cap, 1), wg, wu, wd,
                  _tile(cap, 512), 512)
        y_sorted = _sc_scatter_rows(ye.reshape(N_EXPERTS * cap, D // 2), dstl)
        tb = _tile(T, 256)
        blk_start = jnp.concatenate([off2d.reshape(T)[::tb], jnp.full((1,), N_EXPERTS * cap, jnp.int32)])
        x2d = _combine(h, off2d, cnt2d, blk_start, y_sorted, norm_f.reshape(1, D), tb, 256,
                       final_norm=(l == len(layers) - 1))
    return x2d.reshape(B, S, D)


def kernel(x_prompt, x_sample, norm1, w_in, lam_q1, lam_k1, lam_q2, lam_k2, diff_subln, ln_v_g, ln_v_b,
           w_spatial, b_spatial, gla_decay_w, gla_decay_b, gla_norm, w_out, norm2, w_router, w_gate, w_up,
           w_down, norm_f):
    depth = w_in.shape[0]
    layers = [_prep_layer(l, norm1, w_in, ln_v_g, ln_v_b, w_spatial, b_spatial, gla_decay_w, gla_decay_b,
                          gla_norm, w_out, norm2, w_router, diff_subln) for l in range(depth)]
    ffn_w = (w_gate.astype(BF16), w_up.astype(BF16), w_down.astype(BF16))
    lam_vecs = (lam_q1, lam_k1, lam_q2, lam_k2)
    y_prompt = _trunk(x_prompt, layers, lam_vecs, ffn_w, norm_f)
    y_sample = _trunk(x_sample, layers, lam_vecs, ffn_w, norm_f)
    return (y_prompt, y_sample)
```

```python
import functools
import math

import jax
import jax.numpy as jnp
from jax import lax
from jax.experimental import pallas as pl
from jax.experimental.pallas import tpu as pltpu
from jax.experimental.pallas import tpu_sc as plsc

F32 = jnp.float32
BF16 = jnp.bfloat16

EPS = 1e-6
LOG2E = 1.4426950408889634

A_HEADS = 4
A_QK_DIM = 64
A_V_DIM = 128
A_WIDTH = A_HEADS * A_V_DIM
B_GROUPS = 4
B_WIDTH = 256
B_GROUP_DIM = 64
B_CHUNK = 128
C_HEADS = 4
C_WIDTH = 256
C_V_DIM = 64
C_K_DIM = 32
C_KEY_WIDTH = 128
C_DECAY_RANK = 16
C_GATE_NORMALIZER = 16.0
C_CHUNK = 64
N_EXPERTS = 16
EC_CAPACITY_FACTOR = 2

OFF_AQ, OFF_AK, OFF_AV = 0, 512, 1024
OFF_BU, OFF_BV = 1536, 1792
OFF_CQ, OFF_CK, OFF_CV, OFF_CG, OFF_CZ = 2048, 2176, 2304, 2560, 2816
PROJ_WIDTH = 2848
PROJ_PAD = 2944

VMEM_LIMIT = 56 * 1024 * 1024


def _cparams(sem):
    return pltpu.CompilerParams(dimension_semantics=sem, vmem_limit_bytes=VMEM_LIMIT)


def _gelu_tanh(x):
    return 0.5 * x * (1.0 + jnp.tanh(0.7978845608028654 * (x + 0.044715 * x * x * x)))


def _log_sigmoid(x):
    return jnp.minimum(x, 0.0) - jnp.log(1.0 + jnp.exp(-jnp.abs(x)))


def _inproj_kernel(x_ref, g1_ref, w_ref, lng_ref, lnb_ref, ws_ref, bs_ref, wdec_ref, bdec_ref,
                   aq_ref, ak_ref, av_ref, ob_ref, cq_ref, ck_ref, cv_ref, cg_ref, laf_ref, lab_ref,
                   st_ref, *, tq):
    x = x_ref[...]
    ms = jnp.mean(x * x, axis=-1, keepdims=True)
    hn = (x * lax.rsqrt(ms + EPS) * g1_ref[...]).astype(BF16)
    proj = jnp.dot(hn, w_ref[...], preferred_element_type=F32)

    aq_ref[...] = (proj[:, OFF_AQ:OFF_AK] * (A_QK_DIM ** -0.5 * LOG2E)).astype(BF16)
    ak_ref[...] = proj[:, OFF_AK:OFF_AV].astype(BF16)
    tkb = av_ref.shape[2]
    for c in range(av_ref.shape[0]):
        av_ref[c] = jnp.transpose(proj[c * tkb:(c + 1) * tkb, OFF_AV:OFF_BU]).astype(BF16)
    qs = proj[:, OFF_AQ:OFF_AK] * (A_QK_DIM ** -0.5 * LOG2E)
    kf = proj[:, OFF_AK:OFF_AV]
    grp = (lax.broadcasted_iota(jnp.int32, (A_WIDTH, 128), 0) // A_QK_DIM
           == lax.broadcasted_iota(jnp.int32, (A_WIDTH, 128), 1)).astype(BF16)
    gsum = lambda a: jnp.dot(a.astype(BF16), grp, preferred_element_type=F32)
    qn2, kn2, ss = gsum(qs * qs), gsum(kf * kf), gsum(qs * kf)
    ntile = tkb // tq
    for c in range(av_ref.shape[0]):
        rows = [jnp.max(kn2[c * tkb:(c + 1) * tkb], axis=0, keepdims=True)]
        for t in range(ntile):
            lo = c * tkb + t * tq
            rows.append(jnp.max(qn2[lo:lo + tq], axis=0, keepdims=True))
        for t in range(ntile):
            lo = c * tkb + t * tq
            rows.append(jnp.min(ss[lo:lo + tq], axis=0, keepdims=True))
        rows += [jnp.zeros((1, 128), F32)] * (8 - len(rows))
        st_ref[c] = jnp.concatenate(rows, axis=0)

    u = _gelu_tanh(proj[:, OFF_BU:OFF_BV])
    v = _gelu_tanh(proj[:, OFF_BV:OFF_CQ])
    mu = jnp.mean(v, axis=-1, keepdims=True)
    var = jnp.mean(jnp.square(v - mu), axis=-1, keepdims=True)
    v = ((v - mu) * lax.rsqrt(var + EPS) * lng_ref[...] + lnb_ref[...]).astype(BF16)
    tm = x.shape[0]
    lane = lax.broadcasted_iota(jnp.int32, (B_CHUNK, 128), 1)
    first_half = lane < B_GROUP_DIM
    for c in range(tm // B_CHUNK):
        rows = slice(c * B_CHUNK, (c + 1) * B_CHUNK)
        parts = []
        for p in range(B_GROUPS // 2):
            vch = v[rows, p * 128:(p + 1) * 128]
            m0 = jnp.dot(ws_ref[2 * p], vch, preferred_element_type=F32)
            m1 = jnp.dot(ws_ref[2 * p + 1], vch, preferred_element_type=F32)
            parts.append(jnp.where(first_half, m0, m1))
        mixed = jnp.concatenate(parts, axis=1) + bs_ref[...]
        ob_ref[rows, :] = (u[rows, :] * mixed).astype(BF16)

    cq_ref[...] = proj[:, OFF_CQ:OFF_CK] * (C_K_DIM ** -0.5)
    ck_ref[...] = proj[:, OFF_CK:OFF_CV]
    cv_ref[...] = proj[:, OFF_CV:OFF_CG].astype(BF16)
    cg_ref[...] = proj[:, OFF_CG:OFF_CZ]
    z = proj[:, OFF_CZ:PROJ_PAD].astype(BF16)
    xd = jnp.dot(z, wdec_ref[...], preferred_element_type=F32) + bdec_ref[...]
    la = _log_sigmoid(xd) * (1.0 / C_GATE_NORMALIZER)
    laf_ref[...] = la[:, :C_KEY_WIDTH]
    lab_ref[...] = la[:, C_KEY_WIDTH:]


def _inproj(x2d, g1, w, lng, lnb, ws, bs, wdec, bdec, tm, tk, tq):
    T, D = x2d.shape
    row = lambda n: pl.BlockSpec((tm, n), lambda i: (i, 0))
    full = lambda a: pl.BlockSpec(a.shape, lambda i: (0,) * a.ndim)
    outs = [
        (A_WIDTH, BF16), (A_WIDTH, BF16), None, (B_WIDTH, BF16),
        (C_KEY_WIDTH, F32), (C_KEY_WIDTH, F32), (C_WIDTH, BF16), (C_WIDTH, F32),
        (C_KEY_WIDTH, F32), (C_KEY_WIDTH, F32),
    ]
    vt_spec = pl.BlockSpec((tm // tk, A_WIDTH, tk), lambda i: (i, 0, 0))
    vt_shape = jax.ShapeDtypeStruct((T // tk, A_WIDTH, tk), BF16)
    st_spec = pl.BlockSpec((tm // tk, 8, 128), lambda i: (i, 0, 0))
    st_shape = jax.ShapeDtypeStruct((T // tk, 8, 128), F32)
    return pl.pallas_call(
        functools.partial(_inproj_kernel, tq=tq),
        grid=(T // tm,),
        in_specs=[row(D), full(g1), full(w), full(lng), full(lnb), full(ws), full(bs), full(wdec), full(bdec)],
        out_specs=[vt_spec if o is None else row(o[0]) for o in outs] + [st_spec],
        out_shape=[vt_shape if o is None else jax.ShapeDtypeStruct((T, o[0]), o[1]) for o in outs] + [st_shape],
        compiler_params=_cparams(("parallel",)),
        name="inproj",
    )(x2d, g1, w, lng, lnb, ws, bs, wdec, bdec)


N_POS_LANES = 9


def _split3(x):
    hi = x.astype(BF16)
    r1 = x - hi.astype(F32)
    mid = r1.astype(BF16)
    lo = (r1 - mid.astype(F32)).astype(BF16)
    return hi, mid, lo


def _attn_kernel(lq1_ref, lk1_ref, lq2_ref, lk2_ref, g_ref, kpos_ref, st_ref, q_ref, k_ref, vt_ref, o_ref,
                 qall_ref, *chain_refs, tq, tk, nq, lam_init):
    nch = 2 * nq
    grp = lambda g: chain_refs[g * nch:(g + 1) * nch]
    m_refs, l_refs, acc_refs = grp(0), grp(1), grp(2)
    s_refs, p_refs, al_refs = (grp(3), grp(4)), (grp(5), grp(6)), (grp(7), grp(8))
    h = pl.program_id(1)
    qi = pl.program_id(2)
    nblk = k_ref.shape[1] // tk
    nrest = nblk - 1
    lam = (jnp.exp(jnp.sum(lq1_ref[...] * lk1_ref[...], axis=-1, keepdims=True))
           - jnp.exp(jnp.sum(lq2_ref[...] * lk2_ref[...], axis=-1, keepdims=True)) + lam_init)
    c = jnp.exp2(-8.0 * (h + 1).astype(F32) / A_HEADS) * LOG2E

    lane = lax.broadcasted_iota(jnp.int32, (tq, 128), 1)
    il = lax.broadcasted_iota(jnp.int32, (tq, 128), 0).astype(F32)
    hi, mid, lo = _split3(jnp.where(lane < 3, -c * il, c))
    piece = lane % 3
    qpos = jnp.where(piece == 0, hi, jnp.where(piece == 1, mid, lo))
    qpos = jnp.where(lane < N_POS_LANES, qpos, jnp.zeros_like(qpos))
    chains = [(t, m) for t in range(nq) for m in range(2)]
    for ci, (t, m) in enumerate(chains):
        q = q_ref[0, t * tq:(t + 1) * tq, :]
        keep = (lane < A_QK_DIM) if m == 0 else (lane >= A_QK_DIM)
        qm = jnp.where(keep, q, jnp.zeros_like(q))
        qall_ref[0, ci] = jnp.concatenate([qm, qpos], axis=1)
        qall_ref[1, ci] = jnp.concatenate([qm, -qpos], axis=1)
    kpos = kpos_ref[...]

    def block_of(i):
        above = (i >= qi).astype(jnp.int32)
        return i + above, above

    def scores(j, variant, ci):
        kaug = jnp.concatenate([k_ref[0, pl.ds(pl.multiple_of(j * tk, tk), tk), :], kpos], axis=1)
        return lax.dot_general(kaug, qall_ref[variant, ci], (((1,), (1,)), ((), ())),
                               preferred_element_type=F32)

    def softmax(j, ci, s, first):
        t, _ = chains[ci]
        off = ((qi * nq + t) * tq - j * tk).astype(F32)
        delta = c * jnp.abs(off)
        if first:
            rel = (lax.broadcasted_iota(jnp.int32, (tk, tq), 0)
                   - lax.broadcasted_iota(jnp.int32, (tk, tq), 1)).astype(F32)
            s = s - (2.0 * c) * jnp.maximum(rel - float(t * tq), 0.0)
        bmax = jnp.max(s, axis=0, keepdims=True) - delta
        if first:
            mn = bmax
        else:
            mx = m_refs[ci][...]
            mn = jnp.maximum(mx, bmax)
        p = jnp.exp2(s - (mn + delta))
        psum = jnp.sum(p, axis=0, keepdims=True)
        m_refs[ci][...] = mn
        if first:
            l_refs[ci][...] = psum
            return p.astype(BF16), None
        alpha = jnp.exp2(mx - mn)
        l_refs[ci][...] = alpha * l_refs[ci][...] + psum
        return p.astype(BF16), alpha

    n = nblk
    if n >= 4:
        st = st_ref[...]
        mine = st_ref[qi]
        jv = lax.broadcasted_iota(jnp.int32, (n, 128), 0)
        gl = lax.broadcasted_iota(jnp.int32, (n, 128), 1)
        ub = jnp.full((n, 128), -jnp.inf, F32)
        for t in range(nq):
            i0 = (qi * nq + t) * tq
            gap = jnp.where(jv < qi, i0 - (jv * tk + tk - 1), jv * tk - (i0 + tq - 1))
            dmin = jnp.maximum(gap, 0).astype(F32)
            bound = 1.1 * jnp.sqrt(mine[1 + t:2 + t, :] * st[:, 0, :]) - mine[1 + nq + t:2 + nq + t, :] - c * dmin
            ub = jnp.maximum(ub, bound)
        ub = jnp.where(gl // 2 == h, ub, -jnp.inf)
        live = jnp.logical_or(jnp.max(ub, axis=1, keepdims=True) > -150.0, jv[:, :1] == qi)
        j1 = jv[:, :1]
        jlo = jnp.min(jnp.where(live, j1, n))
        jhi = jnp.max(jnp.where(live, j1, -1))
        cnt = jhi - jlo
        need = jnp.maximum(cnt + 1 - cnt % 2, 3)
        grow_hi = jnp.minimum(need - cnt, (n - 1) - jhi)
        lo = jlo - (need - cnt - grow_hi)
        nvis = need + 1
    else:
        lo = 0
        nvis = n

    def visit(u):
        if isinstance(u, int) and u == 0:
            return qi, 0
        return block_of(lo + u - 1)

    def tick(tau, do_qk, do_sm, do_pv):
        tau, par = tau
        first_sm = isinstance(tau, int) and tau == 1
        first_pv = isinstance(tau, int) and tau == 2
        if do_qk:
            jq, vq = visit(tau)
        if do_sm:
            js, _ = visit(tau - 1)
        if do_pv:
            jp, _ = visit(tau - 2)
        for half in range(0, nch, 2):
            for ci in (half, half + 1):
                if do_qk:
                    s_refs[par][ci][...] = scores(jq, vq, ci)
            for ci in (half, half + 1):
                if do_sm:
                    p, alpha = softmax(js, ci, s_refs[1 - par][ci][...], first_sm)
                    p_refs[1 - par][ci][...] = p
                    if not first_sm:
                        al_refs[1 - par][ci][...] = alpha
            for ci in (half, half + 1):
                if do_pv:
                    pv = jnp.dot(vt_ref[jp], p_refs[par][ci][...], preferred_element_type=F32)
                    if first_pv:
                        acc_refs[ci][...] = pv
                    else:
                        acc_refs[ci][...] = al_refs[par][ci][...] * acc_refs[ci][...] + pv

    assert n == 1 or n % 2 == 0
    tick((0, 0), True, False, False)
    tick((1, 1), n > 1, True, False)
    if n == 1:
        tick((2, 0), False, False, True)
    else:
        tick((2, 0), n > 2, True, True)
        if n > 2:
            tick((3, 1), True, True, True)

            def pair(ip, carry):
                tau = 4 + 2 * ip
                tick((tau, 0), True, True, True)
                tick((tau + 1, 1), True, True, True)
                return carry

            lax.fori_loop(0, (nvis - 4) // 2, pair, 0)
            tick((nvis, 0), False, True, True)
        tick((nvis + 1, 1), False, False, True)

    for t in range(nq):
        a0, a1 = acc_refs[2 * t][...], acc_refs[2 * t + 1][...]
        o = a0 / l_refs[2 * t][...] - lam * (a1 / l_refs[2 * t + 1][...])
        ms = jnp.mean(o * o, axis=0, keepdims=True)
        o = o * lax.rsqrt(ms + EPS) * g_ref[...] * (1.0 - lam_init)
        o_ref[0, t * tq:(t + 1) * tq, :] = jnp.transpose(o).astype(o_ref.dtype)


def _attention(aq, ak, avt, stats, lq1, lk1, lq2, lk2, subln_col, kpos, lam_init, tq, tk):
    B, S, _ = aq.shape
    nq = tk // tq
    nch = 2 * nq
    assert S == tk or (S // tk) % 2 == 0
    vec = pl.BlockSpec((1, A_QK_DIM), lambda b, h, i: (0, 0))
    kern = functools.partial(_attn_kernel, tq=tq, tk=tk, nq=nq, lam_init=lam_init)
    return pl.pallas_call(
        kern,
        grid=(B, A_HEADS, S // tk),
        in_specs=[vec, vec, vec, vec,
                  pl.BlockSpec((A_V_DIM, 1), lambda b, h, i: (0, 0)),
                  pl.BlockSpec((tk, 128), lambda b, h, i: (0, 0)),
                  pl.BlockSpec((S // tk, 8, 128), lambda b, h, i: (b, 0, 0)),
                  pl.BlockSpec((1, tk, 128), lambda b, h, i: (b, i, h)),
                  pl.BlockSpec((1, S, 128), lambda b, h, i: (b, 0, h)),
                  pl.BlockSpec((S // tk, A_V_DIM, tk), lambda b, h, i: (b, h, 0))],
        out_specs=pl.BlockSpec((1, tk, 128), lambda b, h, i: (b, i, h)),
        out_shape=jax.ShapeDtypeStruct((B, S, A_WIDTH), BF16),
        scratch_shapes=([pltpu.VMEM((2, nch, tq, 256), BF16)]
                        + [pltpu.VMEM((1, tq), F32)] * (2 * nch)
                        + [pltpu.VMEM((A_V_DIM, tq), F32)] * nch
                        + [pltpu.VMEM((tk, tq), F32)] * (2 * nch)
                        + [pltpu.VMEM((tk, tq), BF16)] * (2 * nch)
                        + [pltpu.VMEM((1, tq), F32)] * (2 * nch)),
        compiler_params=_cparams(("parallel", "parallel", "arbitrary")),
        name="diff_attn",
    )(lq1, lk1, lq2, lk2, subln_col, kpos, stats, aq, ak, avt)


def _key_pos_lanes(tk):
    j = jnp.arange(tk, dtype=jnp.int32)
    jlo = (j % 256).astype(F32)
    jhi = (j - j % 256).astype(F32)
    cols = [jnp.ones((tk,), F32)] * 3 + [jlo] * 3 + [jhi] * 3
    kp = jnp.stack(cols, axis=1)
    return jnp.pad(kp, ((0, 0), (0, 128 - N_POS_LANES))).astype(BF16)


def _gla_chunk(q, k, v, la, st, reverse):
    C = C_CHUNK
    r = lax.broadcasted_iota(jnp.int32, (C, C), 0)
    c = lax.broadcasted_iota(jnp.int32, (C, C), 1)
    tri = (c >= r) if reverse else (c <= r)
    tri_b = tri.astype(BF16)
    la_hi = la.astype(BF16)
    la_lo = (la - la_hi.astype(F32)).astype(BF16)
    b = (jnp.dot(tri_b, la_hi, preferred_element_type=F32)
         + jnp.dot(tri_b, la_lo, preferred_element_type=F32))
    b_end = b[0:1, :] if reverse else b[C - 1:C, :]
    eb = jnp.exp(b)
    qt = (q * eb).astype(BF16)
    kt = k * jnp.exp(-b)
    kdec = (k * jnp.exp(b_end - b)).astype(BF16)

    lane_k = lax.broadcasted_iota(jnp.int32, (C, C_KEY_WIDTH), 1) // C_K_DIM
    kstack = jnp.concatenate([jnp.where(lane_k == hh, kt, 0.0) for hh in range(C_HEADS)], axis=0).astype(BF16)
    attn = lax.dot_general(qt, kstack, (((1,), (1,)), ((), ())), preferred_element_type=F32)
    ri = lax.broadcasted_iota(jnp.int32, (C, C_HEADS * C), 0)
    cj = lax.broadcasted_iota(jnp.int32, (C, C_HEADS * C), 1) % C
    keep = (cj >= ri) if reverse else (cj <= ri)
    attn = jnp.where(keep, attn, 0.0).astype(BF16)
    lane_v = lax.broadcasted_iota(jnp.int32, (C, C_WIDTH), 1) // C_V_DIM
    vstack = jnp.concatenate([jnp.where(lane_v == hh, v, jnp.zeros_like(v)) for hh in range(C_HEADS)], axis=0)
    o = jnp.dot(attn, vstack, preferred_element_type=F32)
    o = o + lax.dot_general(qt, st.astype(BF16), (((1,), (1,)), ((), ())), preferred_element_type=F32)

    upd = lax.dot_general(v, kdec, (((0,), (0,)), ((), ())), preferred_element_type=F32)
    rh = lax.broadcasted_iota(jnp.int32, (C_WIDTH, C_KEY_WIDTH), 0) // C_V_DIM
    ch = lax.broadcasted_iota(jnp.int32, (C_WIDTH, C_KEY_WIDTH), 1) // C_K_DIM
    st = st * jnp.exp(b_end) + jnp.where(rh == ch, upd, 0.0)
    return o, st


def _gla_fwd_kernel(q_ref, k_ref, v_ref, la_ref, o_ref, st_ref, *, nchunk):
    @pl.when(pl.program_id(1) == 0)
    def _():
        st_ref[...] = jnp.zeros_like(st_ref)

    G = q_ref.shape[0]
    st = [st_ref[g] for g in range(G)]
    for c in range(nchunk):
        rows = slice(c * C_CHUNK, (c + 1) * C_CHUNK)
        for g in range(G):
            o, st[g] = _gla_chunk(q_ref[g, rows, :], k_ref[g, rows, :], v_ref[g, rows, :], la_ref[g, rows, :],
                                  st[g], False)
            o_ref[g, rows, :] = o
    for g in range(G):
        st_ref[g] = st[g]


def _gla_bwd_kernel(q_ref, k_ref, v_ref, la_ref, of_ref, g_ref, gn_ref, o_ref, st_ref, *, nchunk):
    @pl.when(pl.program_id(1) == 0)
    def _():
        st_ref[...] = jnp.zeros_like(st_ref)

    G = q_ref.shape[0]
    st = [st_ref[g] for g in range(G)]
    lane_h = lax.broadcasted_iota(jnp.int32, (C_WIDTH, C_WIDTH), 0) // C_V_DIM
    lane_h2 = lax.broadcasted_iota(jnp.int32, (C_WIDTH, C_WIDTH), 1) // C_V_DIM
    segb = ((lane_h == lane_h2).astype(F32) * (1.0 / C_V_DIM)).astype(BF16)
    for c in reversed(range(nchunk)):
        rows = slice(c * C_CHUNK, (c + 1) * C_CHUNK)
        for g in range(G):
            o, st[g] = _gla_chunk(q_ref[g, rows, :], k_ref[g, rows, :], v_ref[g, rows, :], la_ref[g, rows, :],
                                  st[g], True)
            o = o + of_ref[g, rows, :]
            sq = o * o
            sq_hi = sq.astype(BF16)
            sq_lo = (sq - sq_hi.astype(F32)).astype(BF16)
            ms = (jnp.dot(sq_hi, segb, preferred_element_type=F32)
                  + jnp.dot(sq_lo, segb, preferred_element_type=F32))
            o = o * lax.rsqrt(ms + EPS) * gn_ref[...]
            gate = g_ref[g, rows, :]
            o_ref[g, rows, :] = (o * (gate * jax.nn.sigmoid(gate))).astype(o_ref.dtype)
    for g in range(G):
        st_ref[g] = st[g]


def _gla(cq, ck, cv, cg, laf, lab, gn_tiled, rb):
    B, S, _ = cq.shape
    nb = S // rb
    nchunk = rb // C_CHUNK
    G = 2 if B % 2 == 0 else 1
    fspec = lambda n: pl.BlockSpec((G, rb, n), lambda b, i: (b, i, 0))
    bspec = lambda n: pl.BlockSpec((G, rb, n), lambda b, i: (b, nb - 1 - i, 0))
    st = pltpu.VMEM((G, C_WIDTH, C_KEY_WIDTH), F32)
    o_f = pl.pallas_call(
        functools.partial(_gla_fwd_kernel, nchunk=nchunk),
        grid=(B // G, nb),
        in_specs=[fspec(C_KEY_WIDTH), fspec(C_KEY_WIDTH), fspec(C_WIDTH), fspec(C_KEY_WIDTH)],
        out_specs=fspec(C_WIDTH),
        out_shape=jax.ShapeDtypeStruct((B, S, C_WIDTH), F32),
        scratch_shapes=[st],
        compiler_params=_cparams(("parallel", "arbitrary")),
        name="gla_fwd",
    )(cq, ck, cv, laf)
    return pl.pallas_call(
        functools.partial(_gla_bwd_kernel, nchunk=nchunk),
        grid=(B // G, nb),
        in_specs=[bspec(C_KEY_WIDTH), bspec(C_KEY_WIDTH), bspec(C_WIDTH), bspec(C_KEY_WIDTH),
                  bspec(C_WIDTH), bspec(C_WIDTH), pl.BlockSpec((1, C_WIDTH), lambda b, i: (0, 0))],
        out_specs=bspec(C_WIDTH),
        out_shape=jax.ShapeDtypeStruct((B, S, C_WIDTH), BF16),
        scratch_shapes=[st],
        compiler_params=_cparams(("parallel", "arbitrary")),
        name="gla_bwd",
    )(cq, ck, cv, lab, o_f, cg, gn_tiled)


def _outproj_kernel(x_ref, oa_ref, ob_ref, oc_ref, w_ref, g2_ref, wr_hl_ref, wr_hi_ref,
                    h_ref, hn_ref, aff_ref, *, sub):
    for r in range(x_ref.shape[0] // sub):
        rows = slice(r * sub, (r + 1) * sub)
        mix = jnp.concatenate([oa_ref[rows, :], ob_ref[rows, :], oc_ref[rows, :]], axis=1)
        h = x_ref[rows, :] + jnp.dot(mix, w_ref[...], preferred_element_type=F32)
        h_ref[rows, :] = h
        ms = jnp.mean(h * h, axis=-1, keepdims=True)
        hn = h * lax.rsqrt(ms + EPS) * g2_ref[...]
        hn_ref[rows, :] = _pack_bf16_pairs(hn)
        hn_hi = hn.astype(BF16)
        hn_lo = (hn - hn_hi.astype(F32)).astype(BF16)
        both = jnp.dot(hn_hi, wr_hl_ref[...], preferred_element_type=F32)
        logits = both[:, :128] + both[:, 128:] + jnp.dot(hn_lo, wr_hi_ref[...], preferred_element_type=F32)
        lane = lax.broadcasted_iota(jnp.int32, logits.shape, 1)
        logits = jnp.where(lane < N_EXPERTS, logits, -jnp.inf)
        mx = jnp.max(logits, axis=-1, keepdims=True)
        e = jnp.exp(logits - mx)
        aff = e / jnp.sum(e, axis=-1, keepdims=True)
        aff_ref[:, rows] = jnp.transpose(aff)[:N_EXPERTS, :]


def _outproj(x2d, oa, ob, oc, w, g2, wr_hi, wr_lo, tm):
    T, D = x2d.shape
    wr_hl = jnp.concatenate([wr_hi, wr_lo], axis=1)
    row = lambda n: pl.BlockSpec((tm, n), lambda i: (i, 0))
    full = lambda a: pl.BlockSpec(a.shape, lambda i: (0,) * a.ndim)
    return pl.pallas_call(
        functools.partial(_outproj_kernel, sub=min(tm, 256)),
        grid=(T // tm,),
        in_specs=[row(D), row(A_WIDTH), row(B_WIDTH), row(C_WIDTH), full(w), full(g2), full(wr_hl), full(wr_hi)],
        out_specs=[row(D), row(D // 2), pl.BlockSpec((N_EXPERTS, tm), lambda i: (0, i))],
        out_shape=[jax.ShapeDtypeStruct((T, D), F32), jax.ShapeDtypeStruct((T, D // 2), jnp.uint32),
                   jax.ShapeDtypeStruct((N_EXPERTS, T), F32)],
        compiler_params=_cparams(("parallel",)),
        name="outproj_router",
    )(x2d, oa, ob, oc, w, g2, wr_hl, wr_hi)


def _cumsum_tokens(x, upper, lstrict, ones):
    G, R, _ = x.shape
    xb = x.reshape(G * R, 128).astype(BF16)
    within = jnp.dot(xb, upper, preferred_element_type=F32).reshape(G, R, 128)
    rowtot = jnp.dot(xb, ones, preferred_element_type=F32).reshape(G, R, 128)
    hi = jnp.floor(rowtot * (1.0 / 256.0))
    lo = rowtot - 256.0 * hi
    outs = []
    for g in range(G):
        before = (256.0 * jnp.dot(lstrict, hi[g].astype(BF16), preferred_element_type=F32)
                  + jnp.dot(lstrict, lo[g].astype(BF16), preferred_element_type=F32))
        outs.append(within[g] + before)
    return jnp.stack(outs, axis=0)


def _route_kernel(aff_ref, sel_ref, pos_ref, dst_ref, off_ref, cnt_ref, *, cap):
    E, R, _ = aff_ref.shape
    bits = pltpu.bitcast(aff_ref[...].reshape(E * R, 128), jnp.int32).reshape(E, R, 128)

    def count_ge(v):
        return jnp.sum(jnp.sum((bits >= v).astype(jnp.int32), axis=1, keepdims=True), axis=2, keepdims=True)

    def bisect(i, prefix):
        cand = prefix | jnp.left_shift(jnp.int32(1), 30 - i)
        return jnp.where(count_ge(cand) >= cap, cand, prefix)

    thr = lax.fori_loop(0, 31, bisect, jnp.zeros((E, 1, 1), jnp.int32))
    gt = bits > thr
    eq = bits == thr
    n_gt = jnp.sum(jnp.sum(gt.astype(jnp.int32), axis=1, keepdims=True), axis=2, keepdims=True)
    need_eq = (cap - n_gt).astype(F32)

    r = lax.broadcasted_iota(jnp.int32, (128, 128), 0)
    cc = lax.broadcasted_iota(jnp.int32, (128, 128), 1)
    upper = (r <= cc).astype(BF16)
    ones = jnp.ones((128, 128), BF16)
    rr = lax.broadcasted_iota(jnp.int32, (R, R), 0)
    rc = lax.broadcasted_iota(jnp.int32, (R, R), 1)
    lstrict = (rc < rr).astype(BF16)

    eqf = eq.astype(F32)
    eq_rank = _cumsum_tokens(eqf, upper, lstrict, ones) - eqf
    sel = jnp.logical_or(gt, jnp.logical_and(eq, eq_rank < need_eq))
    self_ = sel.astype(F32)
    pos = _cumsum_tokens(self_, upper, lstrict, ones) - self_
    cnt = jnp.sum(self_, axis=0, keepdims=True)
    off = _cumsum_tokens(cnt, upper, lstrict, ones) - cnt
    rank = jnp.zeros((R, 128), F32)
    for e in range(E):
        dst_ref[e] = (off[0] + rank).astype(jnp.int32)
        rank = rank + self_[e]
    sel_ref[...] = sel.astype(jnp.int32)
    pos_ref[...] = pos.astype(jnp.int32)
    off_ref[...] = off[0].astype(jnp.int32)
    cnt_ref[...] = cnt[0].astype(jnp.int32)


def _route(aff_t, cap):
    E, T = aff_t.shape
    R = T // 128
    a3 = aff_t.reshape(E, R, 128)
    i3 = jax.ShapeDtypeStruct((E, R, 128), jnp.int32)
    i2 = jax.ShapeDtypeStruct((R, 128), jnp.int32)
    return pl.pallas_call(
        functools.partial(_route_kernel, cap=cap),
        out_shape=[i3, i3, i3, i2, i2],
        compiler_params=pltpu.CompilerParams(vmem_limit_bytes=VMEM_LIMIT),
        name="route_select",
    )(a3)


SC_CHUNK = 2048
SC_ROWS = 64


def _sc_mesh():
    return plsc.VectorSubcoreMesh(core_axis_name="c", subcore_axis_name="s")


def _sc_worker():
    info = pltpu.get_tpu_info().sparse_core
    return lax.axis_index("s") * info.num_cores + lax.axis_index("c"), info.num_cores * info.num_subcores


def _sc_compact(sel, pos, dst, aff, E, T, cap):
    @functools.partial(
        pl.kernel, mesh=_sc_mesh(),
        out_type=[jax.ShapeDtypeStruct((E * cap,), jnp.int32), jax.ShapeDtypeStruct((E * cap,), F32),
                  jax.ShapeDtypeStruct((E * cap,), jnp.int32)],
        scratch_types=[pltpu.VMEM((SC_CHUNK,), jnp.int32), pltpu.VMEM((SC_CHUNK,), jnp.int32),
                       pltpu.VMEM((SC_CHUNK,), jnp.int32), pltpu.VMEM((SC_CHUNK,), F32),
                       pltpu.VMEM((cap,), jnp.int32), pltpu.VMEM((cap,), F32), pltpu.VMEM((cap,), jnp.int32)],
        compiler_params=pltpu.CompilerParams(needs_layout_passes=False),
    )
    def k(sel_hbm, pos_hbm, dst_hbm, aff_hbm, idx_hbm, gate_hbm, dstl_hbm,
          sel_v, pos_v, dst_v, aff_v, idx_b, gate_b, dstl_b):
        e, _ = _sc_worker()

        @pl.when(e < E)
        def _():
            @pl.loop(0, T // SC_CHUNK)
            def _(ch):
                base = pl.multiple_of(e * T + ch * SC_CHUNK, 8)
                pltpu.sync_copy(sel_hbm.at[pl.ds(base, SC_CHUNK)], sel_v)
                pltpu.sync_copy(pos_hbm.at[pl.ds(base, SC_CHUNK)], pos_v)
                pltpu.sync_copy(dst_hbm.at[pl.ds(base, SC_CHUNK)], dst_v)
                pltpu.sync_copy(aff_hbm.at[pl.ds(base, SC_CHUNK)], aff_v)

                @pl.loop(0, SC_CHUNK // 16)
                def _(i):
                    sl = pl.ds(pl.multiple_of(i * 16, 16), 16)
                    chosen = sel_v[sl] > 0
                    slot = pos_v[sl]
                    tok = ch * SC_CHUNK + i * 16 + lax.iota(jnp.int32, 16)
                    plsc.store_scatter(idx_b, [slot], tok, mask=chosen)
                    plsc.store_scatter(gate_b, [slot], aff_v[sl], mask=chosen)
                    plsc.store_scatter(dstl_b, [slot], dst_v[sl], mask=chosen)

            out = pl.ds(pl.multiple_of(e * cap, 8), cap)
            pltpu.sync_copy(idx_b, idx_hbm.at[out])
            pltpu.sync_copy(gate_b, gate_hbm.at[out])
            pltpu.sync_copy(dstl_b, dstl_hbm.at[out])

    return k(sel, pos, dst, aff)


def _sc_gather_rows(table, idx):
    B, (_, D) = idx.shape[0], table.shape

    @functools.partial(
        pl.kernel, mesh=_sc_mesh(), out_type=jax.ShapeDtypeStruct((B, D), table.dtype),
        scratch_types=[pltpu.VMEM((SC_ROWS,), jnp.int32), pltpu.VMEM((SC_ROWS, D), table.dtype),
                       pltpu.SemaphoreType.DMA])
    def k(table_hbm, idx_hbm, out_hbm, idx_v, rows_v, sem):
        wid, nw = _sc_worker()
        per = B // nw

        @pl.loop(0, per // SC_ROWS)
        def _(g):
            off = pl.multiple_of(wid * per + g * SC_ROWS, 8)
            pltpu.sync_copy(idx_hbm.at[pl.ds(off, SC_ROWS)], idx_v)
            pltpu.async_copy(table_hbm.at[idx_v], rows_v, sem).wait()
            pltpu.sync_copy(rows_v, out_hbm.at[pl.ds(off, SC_ROWS)])

    return k(table, idx)


def _sc_scatter_rows(rows, dst):
    B, D = rows.shape

    @functools.partial(
        pl.kernel, mesh=_sc_mesh(), out_type=jax.ShapeDtypeStruct((B, D), rows.dtype),
        scratch_types=[pltpu.VMEM((SC_ROWS,), jnp.int32), pltpu.VMEM((SC_ROWS, D), rows.dtype),
                       pltpu.SemaphoreType.DMA])
    def k(rows_hbm, dst_hbm, out_hbm, dst_v, rows_v, sem):
        wid, nw = _sc_worker()
        per = B // nw

        @pl.loop(0, per // SC_ROWS)
        def _(g):
            off = pl.multiple_of(wid * per + g * SC_ROWS, 8)
            pltpu.sync_copy(dst_hbm.at[pl.ds(off, SC_ROWS)], dst_v)
            pltpu.sync_copy(rows_hbm.at[pl.ds(off, SC_ROWS)], rows_v)
            pltpu.async_copy(rows_v, out_hbm.at[dst_v], sem).wait()

    return k(rows, dst)


def _pack_bf16_pairs(x):
    m = x.shape[1] // 2
    bits = pltpu.bitcast(x.astype(BF16).astype(F32), jnp.uint32)
    return (bits[:, m:] & jnp.uint32(0xFFFF0000)) | (bits[:, :m] >> 16)


def _unpack_bf16_pairs(u):
    lo = pltpu.bitcast(u << 16, F32)
    hi = pltpu.bitcast(u & jnp.uint32(0xFFFF0000), F32)
    return jnp.concatenate([lo, hi], axis=1).astype(BF16)


def _combine_kernel(start_ref, h_ref, off_ref, cnt_ref, gf_ref, y_hbm, o_ref, ybuf, sem, acc_ref, nwin_done,
                    *, tb, win, nrows, final_norm):
    b = pl.program_id(0)
    nb = pl.num_programs(0)

    def window_start(blk, w):
        a8 = (start_ref[blk] // 8) * 8
        return pl.multiple_of(jnp.minimum(a8 + w * win, nrows - win), 8)

    def n_windows(blk):
        a8 = (start_ref[blk] // 8) * 8
        return jnp.maximum((start_ref[blk + 1] - a8 + win - 1) // win, 1)

    def copy(blk, w, slot):
        return pltpu.make_async_copy(y_hbm.at[pl.ds(window_start(blk, w), win)], ybuf.at[slot], sem.at[slot])

    def advance(blk, w):
        last = w + 1 >= n_windows(blk)
        return jnp.where(last, blk + 1, blk), jnp.where(last, 0, w + 1)

    nslot = ybuf.shape[0]

    @pl.when(b == 0)
    def _():
        nwin_done[0] = 0
        blk, w = jnp.int32(0), jnp.int32(0)
        for ahead in range(nslot - 1):
            @pl.when(blk < nb)
            def _():
                copy(blk, w, ahead).start()

            blk, w = advance(blk, w)

    rows = off_ref.shape[1]
    off_row = jnp.concatenate([off_ref[0, i:i + 1, :] for i in range(rows)], axis=1)
    end_row = off_row + jnp.concatenate([cnt_ref[0, i:i + 1, :] for i in range(rows)], axis=1)
    acc_ref[...] = jnp.zeros_like(acc_ref)
    nwin = n_windows(b)

    def body(w, carry):
        g = nwin_done[0]
        slot = g % nslot
        copy(b, w, slot).wait()
        blk, wn = b, w
        for _ in range(nslot - 1):
            blk, wn = advance(blk, wn)

        @pl.when(blk < nb)
        def _():
            copy(blk, wn, (g + nslot - 1) % nslot).start()

        first_new = (start_ref[b] // 8) * 8 + w * win
        r = window_start(b, w) + lax.broadcasted_iota(jnp.int32, (win, tb), 0)
        onehot_t = jnp.logical_and(jnp.logical_and(r >= off_row, r < end_row), r >= first_new)
        onehot_t = jnp.where(onehot_t, 1.0, 0.0).astype(BF16)
        y = _unpack_bf16_pairs(ybuf[slot])
        acc_ref[...] += lax.dot_general(onehot_t, y, (((0,), (0,)), ((), ())), preferred_element_type=F32)
        nwin_done[0] = g + 1
        return carry

    lax.fori_loop(0, nwin, body, 0)
    o = h_ref[...] + acc_ref[...]
    if final_norm:
        ms = jnp.mean(o * o, axis=-1, keepdims=True)
        o = o * lax.rsqrt(ms + EPS) * gf_ref[...]
    o_ref[...] = o


def _combine(h, off2d, cnt2d, blk_start, y_sorted, gf, tb, win, final_norm):
    T, D = h.shape
    nrows = y_sorted.shape[0]
    win = min(win, nrows)
    off3 = off2d.reshape(T // tb, tb // 128, 128)
    cnt3 = cnt2d.reshape(T // tb, tb // 128, 128)
    kern = functools.partial(_combine_kernel, tb=tb, win=win, nrows=nrows, final_norm=final_norm)
    grid_spec = pltpu.PrefetchScalarGridSpec(
        num_scalar_prefetch=1,
        grid=(T // tb,),
        in_specs=[pl.BlockSpec((tb, D), lambda i, s: (i, 0)),
                  pl.BlockSpec((1, tb // 128, 128), lambda i, s: (i, 0, 0)),
                  pl.BlockSpec((1, tb // 128, 128), lambda i, s: (i, 0, 0)),
                  pl.BlockSpec((1, D), lambda i, s: (0, 0)),
                  pl.BlockSpec(memory_space=pl.ANY)],
        out_specs=pl.BlockSpec((tb, D), lambda i, s: (i, 0)),
        scratch_shapes=[pltpu.VMEM((3, win, D // 2), jnp.uint32), pltpu.SemaphoreType.DMA((3,)),
                        pltpu.VMEM((tb, D), F32), pltpu.SMEM((1,), jnp.int32)],
    )
    return pl.pallas_call(
        kern, grid_spec=grid_spec, out_shape=jax.ShapeDtypeStruct((T, D), F32),
        compiler_params=_cparams(("arbitrary",)),
        name="combine",
    )(blk_start, h, off3, cnt3, gf, y_sorted)


def _ffn_kernel(x_ref, gate_ref, wg_ref, wu_ref, wd_ref, o_ref, *, fc):
    x = _unpack_bf16_pairs(x_ref[0])
    F = wg_ref.shape[2]
    acc = jnp.zeros((x.shape[0], wd_ref.shape[2]), F32)
    for c in range(F // fc):
        cols = slice(c * fc, (c + 1) * fc)
        g = jnp.dot(x, wg_ref[0, :, cols], preferred_element_type=F32)
        u = jnp.dot(x, wu_ref[0, :, cols], preferred_element_type=F32)
        hid = (g * jax.nn.sigmoid(g) * u).astype(BF16)
        acc = acc + jnp.dot(hid, wd_ref[0, cols, :], preferred_element_type=F32)
    o_ref[0] = _pack_bf16_pairs(acc * gate_ref[0])


def _ffn(xe, gate, wg, wu, wd, tm, fc):
    E, cap, Dh = xe.shape
    D = 2 * Dh
    F = wg.shape[2]
    fc = min(fc, F)
    return pl.pallas_call(
        functools.partial(_ffn_kernel, fc=fc),
        grid=(E, cap // tm),
        in_specs=[pl.BlockSpec((1, tm, Dh), lambda e, i: (e, i, 0)),
                  pl.BlockSpec((1, tm, 1), lambda e, i: (e, i, 0)),
                  pl.BlockSpec((1, D, F), lambda e, i: (e, 0, 0)),
                  pl.BlockSpec((1, D, F), lambda e, i: (e, 0, 0)),
                  pl.BlockSpec((1, F, D), lambda e, i: (e, 0, 0))],
        out_specs=pl.BlockSpec((1, tm, Dh), lambda e, i: (e, i, 0)),
        out_shape=jax.ShapeDtypeStruct((E, cap, Dh), jnp.uint32),
        compiler_params=_cparams(("parallel", "arbitrary")),
        name="expert_ffn",
    )(xe, gate, wg, wu, wd)


def _prep_layer(l, norm1, w_in, ln_v_g, ln_v_b, w_spatial, b_spatial, gla_decay_w, gla_decay_b, gla_norm,
                w_out, norm2, w_router, diff_subln):
    D = w_in.shape[1]
    w = jnp.pad(w_in[l], ((0, 0), (0, PROJ_PAD - PROJ_WIDTH))).astype(BF16)
    bs = jnp.repeat(b_spatial[l].T, B_GROUP_DIM, axis=1)
    wdec = jnp.zeros((128, 2 * C_KEY_WIDTH), F32)
    wdec = wdec.at[:C_DECAY_RANK, :C_KEY_WIDTH].set(gla_decay_w[l, 0])
    wdec = wdec.at[C_DECAY_RANK:2 * C_DECAY_RANK, C_KEY_WIDTH:].set(gla_decay_w[l, 1])
    bdec = gla_decay_b[l].reshape(1, 2 * C_KEY_WIDTH)
    wr = jnp.pad(w_router[l], ((0, 0), (0, 128 - N_EXPERTS)))
    wr_hi = wr.astype(BF16)
    wr_lo = (wr - wr_hi.astype(F32)).astype(BF16)
    return dict(
        g1=norm1[l].reshape(1, D), w=w, lng=ln_v_g[l].reshape(1, B_WIDTH), lnb=ln_v_b[l].reshape(1, B_WIDTH),
        ws=w_spatial[l].astype(BF16), bs=bs, wdec=wdec.astype(BF16), bdec=bdec,
        gn=jnp.tile(gla_norm[l], C_HEADS).reshape(1, C_WIDTH), w_out=w_out[l].astype(BF16),
        g2=norm2[l].reshape(1, D), wr_hi=wr_hi, wr_lo=wr_lo, subln=diff_subln[l].reshape(A_V_DIM, 1),
    )


def _tile(n, pref):
    t = min(n, pref)
    while n % t:
        t //= 2
    return t


def _trunk(x, layers, lam_vecs, ffn_w, norm_f):
    B, S, D = x.shape
    T = B * S
    cap = EC_CAPACITY_FACTOR * T // N_EXPERTS
    tm = _tile(T, 512)
    tk = _tile(S, 512)
    tq = min(tk, 256)
    kpos = _key_pos_lanes(tk)
    x2d = x.reshape(T, D)
    for l, p in enumerate(layers):
        lam_init = 0.8 - 0.6 * math.exp(-0.3 * l)
        aq, ak, avt, ob, cq, ck, cv, cg, laf, lab, stats = _inproj(
            x2d, p["g1"], p["w"], p["lng"], p["lnb"], p["ws"], p["bs"], p["wdec"], p["bdec"], tm, min(tk, tm), tq)
        r3 = lambda a: a.reshape(B, S, a.shape[-1])
        lq1, lk1, lq2, lk2 = (v[l:l + 1] for v in lam_vecs)
        oa = _attention(r3(aq), r3(ak), avt, stats, lq1, lk1, lq2, lk2, p["subln"], kpos, lam_init, tq, tk)
        oc = _gla(r3(cq), r3(ck), r3(cv), r3(cg), r3(laf), r3(lab), p["gn"], _tile(S, 512))
        h, hn, aff_t = _outproj(x2d, oa.reshape(T, A_WIDTH), ob, oc.reshape(T, C_WIDTH),
                                p["w_out"], p["g2"], p["wr_hi"], p["wr_lo"], tm)
        sel, pos, dst, off2d, cnt2d = _route(aff_t, cap)
        flat = lambda a: a.reshape(N_EXPERTS * T)
        idx, gate, dstl = _sc_compact(flat(sel), flat(pos), flat(dst), flat(aff_t), N_EXPERTS, T, cap)
        xe = _sc_gather_rows(hn, idx)
        wg, wu, wd = (w[l] for w in ffn_w)
        ye = _ffn(xe.reshape(N_EXPERTS, cap, D // 2), gate.reshape(N_EXPERTS, cap, 1), wg, wu, wd,
                  _tile(cap, 512), 512)
        y_sorted = _sc_scatter_rows(ye.reshape(N_EXPERTS * cap, D // 2), dstl)
        tb = _tile(T, 512)
        blk_start = jnp.concatenate([off2d.reshape(T)[::tb], jnp.full((4,), N_EXPERTS * cap, jnp.int32)])
        x2d = _combine(h, off2d, cnt2d, blk_start, y_sorted, norm_f.reshape(1, D), tb, 512,
                       final_norm=(l == len(layers) - 1))
    return x2d.reshape(B, S, D)


def kernel(x_prompt, x_sample, norm1, w_in, lam_q1, lam_k1, lam_q2, lam_k2, diff_subln, ln_v_g, ln_v_b,
           w_spatial, b_spatial, gla_decay_w, gla_decay_b, gla_norm, w_out, norm2, w_router, w_gate, w_up,
           w_down, norm_f):
    depth = w_in.shape[0]
    layers = [_prep_layer(l, norm1, w_in, ln_v_g, ln_v_b, w_spatial, b_spatial, gla_decay_w, gla_decay_b,
                          gla_norm, w_out, norm2, w_router, diff_subln) for l in range(depth)]
    ffn_w = (w_gate.astype(BF16), w_up.astype(BF16), w_down.astype(BF16))
    lam_vecs = (lam_q1, lam_k1, lam_q2, lam_k2)
    y_prompt = _trunk(x_prompt, layers, lam_vecs, ffn_w, norm_f)
    y_sample = _trunk(x_sample, layers, lam_vecs, ffn_w, norm_f)
    return (y_prompt, y_sample)
```

```python
import functools
import math

import jax
import jax.numpy as jnp
from jax import lax
from jax.experimental import pallas as pl
from jax.experimental.pallas import tpu as pltpu
from jax.experimental.pallas import tpu_sc as plsc

F32 = jnp.float32
BF16 = jnp.bfloat16

EPS = 1e-6
LOG2E = 1.4426950408889634

A_HEADS = 4
A_QK_DIM = 64
A_V_DIM = 128
A_WIDTH = A_HEADS * A_V_DIM
B_GROUPS = 4
B_WIDTH = 256
B_GROUP_DIM = 64
B_CHUNK = 128
C_HEADS = 4
C_WIDTH = 256
C_V_DIM = 64
C_K_DIM = 32
C_KEY_WIDTH = 128
C_DECAY_RANK = 16
C_GATE_NORMALIZER = 16.0
C_CHUNK = 64
N_EXPERTS = 16
EC_CAPACITY_FACTOR = 2

OFF_AQ, OFF_AK, OFF_AV = 0, 512, 1024
OFF_BU, OFF_BV = 1536, 1792
OFF_CQ, OFF_CK, OFF_CV, OFF_CG, OFF_CZ = 2048, 2176, 2304, 2560, 2816
PROJ_WIDTH = 2848
PROJ_PAD = 2944

VMEM_LIMIT = 56 * 1024 * 1024


def _cparams(sem):
    return pltpu.CompilerParams(dimension_semantics=sem, vmem_limit_bytes=VMEM_LIMIT)


def _gelu_tanh(x):
    return 0.5 * x * (1.0 + jnp.tanh(0.7978845608028654 * (x + 0.044715 * x * x * x)))


def _log_sigmoid(x):
    return jnp.minimum(x, 0.0) - jnp.log(1.0 + jnp.exp(-jnp.abs(x)))


def _inproj_kernel(x_ref, g1_ref, w_ref, lng_ref, lnb_ref, ws_ref, bs_ref, wdec_ref, bdec_ref,
                   aq_ref, ak_ref, av_ref, ob_ref, cq_ref, ck_ref, cv_ref, cg_ref, laf_ref, lab_ref,
                   st_ref, *, tq):
    x = x_ref[...]
    ms = jnp.mean(x * x, axis=-1, keepdims=True)
    hn = (x * lax.rsqrt(ms + EPS) * g1_ref[...]).astype(BF16)
    proj = jnp.dot(hn, w_ref[...], preferred_element_type=F32)

    aq_ref[...] = (proj[:, OFF_AQ:OFF_AK] * (A_QK_DIM ** -0.5 * LOG2E)).astype(BF16)
    ak_ref[...] = proj[:, OFF_AK:OFF_AV].astype(BF16)
    tkb = av_ref.shape[2]
    for c in range(av_ref.shape[0]):
        av_ref[c] = jnp.transpose(proj[c * tkb:(c + 1) * tkb, OFF_AV:OFF_BU]).astype(BF16)
    qs = proj[:, OFF_AQ:OFF_AK] * (A_QK_DIM ** -0.5 * LOG2E)
    kf = proj[:, OFF_AK:OFF_AV]
    grp = (lax.broadcasted_iota(jnp.int32, (A_WIDTH, 128), 0) // A_QK_DIM
           == lax.broadcasted_iota(jnp.int32, (A_WIDTH, 128), 1)).astype(BF16)
    gsum = lambda a: jnp.dot(a.astype(BF16), grp, preferred_element_type=F32)
    qn2, kn2, ss = gsum(qs * qs), gsum(kf * kf), gsum(qs * kf)
    ntile = tkb // tq
    for c in range(av_ref.shape[0]):
        rows = [jnp.max(kn2[c * tkb:(c + 1) * tkb], axis=0, keepdims=True)]
        for t in range(ntile):
            lo = c * tkb + t * tq
            rows.append(jnp.max(qn2[lo:lo + tq], axis=0, keepdims=True))
        for t in range(ntile):
            lo = c * tkb + t * tq
            rows.append(jnp.min(ss[lo:lo + tq], axis=0, keepdims=True))
        rows += [jnp.zeros((1, 128), F32)] * (8 - len(rows))
        st_ref[c] = jnp.concatenate(rows, axis=0)

    u = _gelu_tanh(proj[:, OFF_BU:OFF_BV])
    v = _gelu_tanh(proj[:, OFF_BV:OFF_CQ])
    mu = jnp.mean(v, axis=-1, keepdims=True)
    var = jnp.mean(jnp.square(v - mu), axis=-1, keepdims=True)
    v = ((v - mu) * lax.rsqrt(var + EPS) * lng_ref[...] + lnb_ref[...]).astype(BF16)
    tm = x.shape[0]
    lane = lax.broadcasted_iota(jnp.int32, (B_CHUNK, 128), 1)
    first_half = lane < B_GROUP_DIM
    for c in range(tm // B_CHUNK):
        rows = slice(c * B_CHUNK, (c + 1) * B_CHUNK)
        parts = []
        for p in range(B_GROUPS // 2):
            vch = v[rows, p * 128:(p + 1) * 128]
            m0 = jnp.dot(ws_ref[2 * p], vch, preferred_element_type=F32)
            m1 = jnp.dot(ws_ref[2 * p + 1], vch, preferred_element_type=F32)
            parts.append(jnp.where(first_half, m0, m1))
        mixed = jnp.concatenate(parts, axis=1) + bs_ref[...]
        ob_ref[rows, :] = (u[rows, :] * mixed).astype(BF16)

    cq_ref[...] = proj[:, OFF_CQ:OFF_CK] * (C_K_DIM ** -0.5)
    ck_ref[...] = proj[:, OFF_CK:OFF_CV]
    cv_ref[...] = proj[:, OFF_CV:OFF_CG].astype(BF16)
    cg_ref[...] = proj[:, OFF_CG:OFF_CZ]
    z = proj[:, OFF_CZ:PROJ_PAD].astype(BF16)
    xd = jnp.dot(z, wdec_ref[...], preferred_element_type=F32) + bdec_ref[...]
    la = _log_sigmoid(xd) * (1.0 / C_GATE_NORMALIZER)
    laf_ref[...] = la[:, :C_KEY_WIDTH]
    lab_ref[...] = la[:, C_KEY_WIDTH:]


def _inproj(x2d, g1, w, lng, lnb, ws, bs, wdec, bdec, tm, tk, tq):
    T, D = x2d.shape
    row = lambda n: pl.BlockSpec((tm, n), lambda i: (i, 0))
    full = lambda a: pl.BlockSpec(a.shape, lambda i: (0,) * a.ndim)
    outs = [
        (A_WIDTH, BF16), (A_WIDTH, BF16), None, (B_WIDTH, BF16),
        (C_KEY_WIDTH, F32), (C_KEY_WIDTH, F32), (C_WIDTH, BF16), (C_WIDTH, F32),
        (C_KEY_WIDTH, F32), (C_KEY_WIDTH, F32),
    ]
    vt_spec = pl.BlockSpec((tm // tk, A_WIDTH, tk), lambda i: (i, 0, 0))
    vt_shape = jax.ShapeDtypeStruct((T // tk, A_WIDTH, tk), BF16)
    st_spec = pl.BlockSpec((tm // tk, 8, 128), lambda i: (i, 0, 0))
    st_shape = jax.ShapeDtypeStruct((T // tk, 8, 128), F32)
    return pl.pallas_call(
        functools.partial(_inproj_kernel, tq=tq),
        grid=(T // tm,),
        in_specs=[row(D), full(g1), full(w), full(lng), full(lnb), full(ws), full(bs), full(wdec), full(bdec)],
        out_specs=[vt_spec if o is None else row(o[0]) for o in outs] + [st_spec],
        out_shape=[vt_shape if o is None else jax.ShapeDtypeStruct((T, o[0]), o[1]) for o in outs] + [st_shape],
        compiler_params=_cparams(("parallel",)),
        name="inproj",
    )(x2d, g1, w, lng, lnb, ws, bs, wdec, bdec)


N_POS_LANES = 9


def _split3(x):
    hi = x.astype(BF16)
    r1 = x - hi.astype(F32)
    mid = r1.astype(BF16)
    lo = (r1 - mid.astype(F32)).astype(BF16)
    return hi, mid, lo


def _attn_kernel(lq1_ref, lk1_ref, lq2_ref, lk2_ref, g_ref, kpos_ref, st_ref, q_ref, k_ref, vt_ref, o_ref,
                 qall_ref, *chain_refs, tq, tk, nq, lam_init):
    nch = 2 * nq
    grp = lambda g: chain_refs[g * nch:(g + 1) * nch]
    m_refs, l_refs, acc_refs = grp(0), grp(1), grp(2)
    s_refs = (grp(3), grp(4))
    h = pl.program_id(1)
    qi = pl.program_id(2)
    nblk = k_ref.shape[1] // tk
    nrest = nblk - 1
    lam = (jnp.exp(jnp.sum(lq1_ref[...] * lk1_ref[...], axis=-1, keepdims=True))
           - jnp.exp(jnp.sum(lq2_ref[...] * lk2_ref[...], axis=-1, keepdims=True)) + lam_init)
    c = jnp.exp2(-8.0 * (h + 1).astype(F32) / A_HEADS) * LOG2E

    lane = lax.broadcasted_iota(jnp.int32, (tq, 128), 1)
    il = lax.broadcasted_iota(jnp.int32, (tq, 128), 0).astype(F32)
    hi, mid, lo = _split3(jnp.where(lane < 3, -c * il, c))
    piece = lane % 3
    qpos = jnp.where(piece == 0, hi, jnp.where(piece == 1, mid, lo))
    qpos = jnp.where(lane < N_POS_LANES, qpos, jnp.zeros_like(qpos))
    chains = [(t, m) for t in range(nq) for m in range(2)]
    for ci, (t, m) in enumerate(chains):
        q = q_ref[0, t * tq:(t + 1) * tq, :]
        keep = (lane < A_QK_DIM) if m == 0 else (lane >= A_QK_DIM)
        qm = jnp.where(keep, q, jnp.zeros_like(q))
        qall_ref[0, ci] = jnp.concatenate([qm, qpos], axis=1)
        qall_ref[1, ci] = jnp.concatenate([qm, -qpos], axis=1)
    kpos = kpos_ref[...]

    def block_of(i):
        above = (i >= qi).astype(jnp.int32)
        return i + above, above

    def scores(j, variant, ci):
        kaug = jnp.concatenate([k_ref[0, pl.ds(pl.multiple_of(j * tk, tk), tk), :], kpos], axis=1)
        return lax.dot_general(kaug, qall_ref[variant, ci], (((1,), (1,)), ((), ())),
                               preferred_element_type=F32)

    def softmax(j, ci, s, first):
        t, _ = chains[ci]
        off = ((qi * nq + t) * tq - j * tk).astype(F32)
        delta = c * jnp.abs(off)
        if first:
            rel = (lax.broadcasted_iota(jnp.int32, (tk, tq), 0)
                   - lax.broadcasted_iota(jnp.int32, (tk, tq), 1)).astype(F32)
            s = s - (2.0 * c) * jnp.maximum(rel - float(t * tq), 0.0)
        bmax = jnp.max(s, axis=0, keepdims=True) - delta
        if first:
            mn = bmax
        else:
            mx = m_refs[ci][...]
            mn = jnp.maximum(mx, bmax)
        p = jnp.exp2(s - (mn + delta))
        psum = jnp.sum(p, axis=0, keepdims=True)
        m_refs[ci][...] = mn
        if first:
            l_refs[ci][...] = psum
            return p.astype(BF16), None
        alpha = jnp.exp2(mx - mn)
        l_refs[ci][...] = alpha * l_refs[ci][...] + psum
        return p.astype(BF16), alpha

    n = nblk
    if n >= 4:
        st = st_ref[...]
        mine = st_ref[qi]
        jv = lax.broadcasted_iota(jnp.int32, (n, 128), 0)
        gl = lax.broadcasted_iota(jnp.int32, (n, 128), 1)
        ub = jnp.full((n, 128), -jnp.inf, F32)
        for t in range(nq):
            i0 = (qi * nq + t) * tq
            gap = jnp.where(jv < qi, i0 - (jv * tk + tk - 1), jv * tk - (i0 + tq - 1))
            dmin = jnp.maximum(gap, 0).astype(F32)
            bound = 1.1 * jnp.sqrt(mine[1 + t:2 + t, :] * st[:, 0, :]) - mine[1 + nq + t:2 + nq + t, :] - c * dmin
            ub = jnp.maximum(ub, bound)
        ub = jnp.where(gl // 2 == h, ub, -jnp.inf)
        live = jnp.logical_or(jnp.max(ub, axis=1, keepdims=True) > -150.0, jv[:, :1] == qi)
        j1 = jv[:, :1]
        jlo = jnp.min(jnp.where(live, j1, n))
        jhi = jnp.max(jnp.where(live, j1, -1))
        cnt = jhi - jlo
        need = cnt + 1 - cnt % 2
        grow_hi = jnp.minimum(need - cnt, (n - 1) - jhi)
        lo = jlo - (need - cnt - grow_hi)
        nvis = need + 1
    else:
        lo = 0
        nvis = n

    def visit(u):
        if isinstance(u, int) and u == 0:
            return qi, 0
        return block_of(lo + u - 1)

    def tick(tau, do_qk, do_sm):
        tau, par = tau
        first = isinstance(tau, int) and tau == 1
        if do_qk:
            jq, vq = visit(tau)
        if do_sm:
            js, _ = visit(tau - 1)
        for half in range(0, nch, 2):
            for ci in (half, half + 1):
                if do_qk:
                    s_refs[par][ci][...] = scores(jq, vq, ci)
            for ci in (half, half + 1):
                if do_sm:
                    p, alpha = softmax(js, ci, s_refs[1 - par][ci][...], first)
                    pv = jnp.dot(vt_ref[js], p, preferred_element_type=F32)
                    if first:
                        acc_refs[ci][...] = pv
                    else:
                        acc_refs[ci][...] = alpha * acc_refs[ci][...] + pv

    assert n == 1 or n % 2 == 0
    tick((0, 0), True, False)
    tick((1, 1), n > 1, True)
    if n > 1:
        def pair(ip, carry):
            tau = 2 + 2 * ip
            tick((tau, 0), True, True)
            tick((tau + 1, 1), True, True)
            return carry

        lax.fori_loop(0, (nvis - 2) // 2, pair, 0)
        tick((nvis, 0), False, True)

    for t in range(nq):
        a0, a1 = acc_refs[2 * t][...], acc_refs[2 * t + 1][...]
        o = a0 / l_refs[2 * t][...] - lam * (a1 / l_refs[2 * t + 1][...])
        ms = jnp.mean(o * o, axis=0, keepdims=True)
        o = o * lax.rsqrt(ms + EPS) * g_ref[...] * (1.0 - lam_init)
        o_ref[0, t * tq:(t + 1) * tq, :] = jnp.transpose(o).astype(o_ref.dtype)


def _attention(aq, ak, avt, stats, lq1, lk1, lq2, lk2, subln_col, kpos, lam_init, tq, tk):
    B, S, _ = aq.shape
    nq = tk // tq
    nch = 2 * nq
    assert S == tk or (S // tk) % 2 == 0
    vec = pl.BlockSpec((1, A_QK_DIM), lambda b, h, i: (0, 0))
    kern = functools.partial(_attn_kernel, tq=tq, tk=tk, nq=nq, lam_init=lam_init)
    return pl.pallas_call(
        kern,
        grid=(B, A_HEADS, S // tk),
        in_specs=[vec, vec, vec, vec,
                  pl.BlockSpec((A_V_DIM, 1), lambda b, h, i: (0, 0)),
                  pl.BlockSpec((tk, 128), lambda b, h, i: (0, 0)),
                  pl.BlockSpec((S // tk, 8, 128), lambda b, h, i: (b, 0, 0)),
                  pl.BlockSpec((1, tk, 128), lambda b, h, i: (b, i, h)),
                  pl.BlockSpec((1, S, 128), lambda b, h, i: (b, 0, h)),
                  pl.BlockSpec((S // tk, A_V_DIM, tk), lambda b, h, i: (b, h, 0))],
        out_specs=pl.BlockSpec((1, tk, 128), lambda b, h, i: (b, i, h)),
        out_shape=jax.ShapeDtypeStruct((B, S, A_WIDTH), BF16),
        scratch_shapes=([pltpu.VMEM((2, nch, tq, 256), BF16)]
                        + [pltpu.VMEM((1, tq), F32)] * (2 * nch)
                        + [pltpu.VMEM((A_V_DIM, tq), F32)] * nch
                        + [pltpu.VMEM((tk, tq), F32)] * (2 * nch)),
        compiler_params=_cparams(("parallel", "parallel", "arbitrary")),
        name="diff_attn",
    )(lq1, lk1, lq2, lk2, subln_col, kpos, stats, aq, ak, avt)


def _key_pos_lanes(tk):
    j = jnp.arange(tk, dtype=jnp.int32)
    jlo = (j % 256).astype(F32)
    jhi = (j - j % 256).astype(F32)
    cols = [jnp.ones((tk,), F32)] * 3 + [jlo] * 3 + [jhi] * 3
    kp = jnp.stack(cols, axis=1)
    return jnp.pad(kp, ((0, 0), (0, 128 - N_POS_LANES))).astype(BF16)


def _gla_chunk(q, k, v, la, st, reverse):
    C = C_CHUNK
    r = lax.broadcasted_iota(jnp.int32, (C, C), 0)
    c = lax.broadcasted_iota(jnp.int32, (C, C), 1)
    tri = (c >= r) if reverse else (c <= r)
    tri_b = tri.astype(BF16)
    la_hi = la.astype(BF16)
    la_lo = (la - la_hi.astype(F32)).astype(BF16)
    b = (jnp.dot(tri_b, la_hi, preferred_element_type=F32)
         + jnp.dot(tri_b, la_lo, preferred_element_type=F32))
    b_end = b[0:1, :] if reverse else b[C - 1:C, :]
    eb = jnp.exp(b)
    qt = (q * eb).astype(BF16)
    kt = k * jnp.exp(-b)
    kdec = (k * jnp.exp(b_end - b)).astype(BF16)

    lane_k = lax.broadcasted_iota(jnp.int32, (C, C_KEY_WIDTH), 1) // C_K_DIM
    kstack = jnp.concatenate([jnp.where(lane_k == hh, kt, 0.0) for hh in range(C_HEADS)], axis=0).astype(BF16)
    attn = lax.dot_general(qt, kstack, (((1,), (1,)), ((), ())), preferred_element_type=F32)
    ri = lax.broadcasted_iota(jnp.int32, (C, C_HEADS * C), 0)
    cj = lax.broadcasted_iota(jnp.int32, (C, C_HEADS * C), 1) % C
    keep = (cj >= ri) if reverse else (cj <= ri)
    attn = jnp.where(keep, attn, 0.0).astype(BF16)
    lane_v = lax.broadcasted_iota(jnp.int32, (C, C_WIDTH), 1) // C_V_DIM
    vstack = jnp.concatenate([jnp.where(lane_v == hh, v, jnp.zeros_like(v)) for hh in range(C_HEADS)], axis=0)
    o = jnp.dot(attn, vstack, preferred_element_type=F32)
    o = o + lax.dot_general(qt, st.astype(BF16), (((1,), (1,)), ((), ())), preferred_element_type=F32)

    upd = lax.dot_general(v, kdec, (((0,), (0,)), ((), ())), preferred_element_type=F32)
    rh = lax.broadcasted_iota(jnp.int32, (C_WIDTH, C_KEY_WIDTH), 0) // C_V_DIM
    ch = lax.broadcasted_iota(jnp.int32, (C_WIDTH, C_KEY_WIDTH), 1) // C_K_DIM
    st = st * jnp.exp(b_end) + jnp.where(rh == ch, upd, 0.0)
    return o, st


def _gla_fwd_kernel(q_ref, k_ref, v_ref, la_ref, o_ref, st_ref, *, nchunk):
    @pl.when(pl.program_id(1) == 0)
    def _():
        st_ref[...] = jnp.zeros_like(st_ref)

    G = q_ref.shape[0]
    st = [st_ref[g] for g in range(G)]
    for c in range(nchunk):
        rows = slice(c * C_CHUNK, (c + 1) * C_CHUNK)
        for g in range(G):
            o, st[g] = _gla_chunk(q_ref[g, rows, :], k_ref[g, rows, :], v_ref[g, rows, :], la_ref[g, rows, :],
                                  st[g], False)
            o_ref[g, rows, :] = o
    for g in range(G):
        st_ref[g] = st[g]


def _gla_bwd_kernel(q_ref, k_ref, v_ref, la_ref, of_ref, g_ref, gn_ref, o_ref, st_ref, *, nchunk):
    @pl.when(pl.program_id(1) == 0)
    def _():
        st_ref[...] = jnp.zeros_like(st_ref)

    G = q_ref.shape[0]
    st = [st_ref[g] for g in range(G)]
    lane_h = lax.broadcasted_iota(jnp.int32, (C_WIDTH, C_WIDTH), 0) // C_V_DIM
    lane_h2 = lax.broadcasted_iota(jnp.int32, (C_WIDTH, C_WIDTH), 1) // C_V_DIM
    segb = ((lane_h == lane_h2).astype(F32) * (1.0 / C_V_DIM)).astype(BF16)
    for c in reversed(range(nchunk)):
        rows = slice(c * C_CHUNK, (c + 1) * C_CHUNK)
        for g in range(G):
            o, st[g] = _gla_chunk(q_ref[g, rows, :], k_ref[g, rows, :], v_ref[g, rows, :], la_ref[g, rows, :],
                                  st[g], True)
            o = o + of_ref[g, rows, :]
            sq = o * o
            sq_hi = sq.astype(BF16)
            sq_lo = (sq - sq_hi.astype(F32)).astype(BF16)
            ms = (jnp.dot(sq_hi, segb, preferred_element_type=F32)
                  + jnp.dot(sq_lo, segb, preferred_element_type=F32))
            o = o * lax.rsqrt(ms + EPS) * gn_ref[...]
            gate = g_ref[g, rows, :]
            o_ref[g, rows, :] = (o * (gate * jax.nn.sigmoid(gate))).astype(o_ref.dtype)
    for g in range(G):
        st_ref[g] = st[g]


def _gla(cq, ck, cv, cg, laf, lab, gn_tiled, rb):
    B, S, _ = cq.shape
    nb = S // rb
    nchunk = rb // C_CHUNK
    G = 2 if B % 2 == 0 else 1
    fspec = lambda n: pl.BlockSpec((G, rb, n), lambda b, i: (b, i, 0))
    bspec = lambda n: pl.BlockSpec((G, rb, n), lambda b, i: (b, nb - 1 - i, 0))
    st = pltpu.VMEM((G, C_WIDTH, C_KEY_WIDTH), F32)
    o_f = pl.pallas_call(
        functools.partial(_gla_fwd_kernel, nchunk=nchunk),
        grid=(B // G, nb),
        in_specs=[fspec(C_KEY_WIDTH), fspec(C_KEY_WIDTH), fspec(C_WIDTH), fspec(C_KEY_WIDTH)],
        out_specs=fspec(C_WIDTH),
        out_shape=jax.ShapeDtypeStruct((B, S, C_WIDTH), F32),
        scratch_shapes=[st],
        compiler_params=_cparams(("parallel", "arbitrary")),
        name="gla_fwd",
    )(cq, ck, cv, laf)
    return pl.pallas_call(
        functools.partial(_gla_bwd_kernel, nchunk=nchunk),
        grid=(B // G, nb),
        in_specs=[bspec(C_KEY_WIDTH), bspec(C_KEY_WIDTH), bspec(C_WIDTH), bspec(C_KEY_WIDTH),
                  bspec(C_WIDTH), bspec(C_WIDTH), pl.BlockSpec((1, C_WIDTH), lambda b, i: (0, 0))],
        out_specs=bspec(C_WIDTH),
        out_shape=jax.ShapeDtypeStruct((B, S, C_WIDTH), BF16),
        scratch_shapes=[st],
        compiler_params=_cparams(("parallel", "arbitrary")),
        name="gla_bwd",
    )(cq, ck, cv, lab, o_f, cg, gn_tiled)


def _outproj_kernel(x_ref, oa_ref, ob_ref, oc_ref, w_ref, g2_ref, wr_hl_ref, wr_hi_ref,
                    h_ref, hn_ref, aff_ref, *, sub):
    for r in range(x_ref.shape[0] // sub):
        rows = slice(r * sub, (r + 1) * sub)
        mix = jnp.concatenate([oa_ref[rows, :], ob_ref[rows, :], oc_ref[rows, :]], axis=1)
        h = x_ref[rows, :] + jnp.dot(mix, w_ref[...], preferred_element_type=F32)
        h_ref[rows, :] = h
        ms = jnp.mean(h * h, axis=-1, keepdims=True)
        hn = h * lax.rsqrt(ms + EPS) * g2_ref[...]
        hn_ref[rows, :] = _pack_bf16_pairs(hn)
        hn_hi = hn.astype(BF16)
        hn_lo = (hn - hn_hi.astype(F32)).astype(BF16)
        both = jnp.dot(hn_hi, wr_hl_ref[...], preferred_element_type=F32)
        logits = both[:, :128] + both[:, 128:] + jnp.dot(hn_lo, wr_hi_ref[...], preferred_element_type=F32)
        lane = lax.broadcasted_iota(jnp.int32, logits.shape, 1)
        logits = jnp.where(lane < N_EXPERTS, logits, -jnp.inf)
        mx = jnp.max(logits, axis=-1, keepdims=True)
        e = jnp.exp(logits - mx)
        aff = e / jnp.sum(e, axis=-1, keepdims=True)
        aff_ref[:, rows] = jnp.transpose(aff)[:N_EXPERTS, :]


def _outproj(x2d, oa, ob, oc, w, g2, wr_hi, wr_lo, tm):
    T, D = x2d.shape
    wr_hl = jnp.concatenate([wr_hi, wr_lo], axis=1)
    row = lambda n: pl.BlockSpec((tm, n), lambda i: (i, 0))
    full = lambda a: pl.BlockSpec(a.shape, lambda i: (0,) * a.ndim)
    return pl.pallas_call(
        functools.partial(_outproj_kernel, sub=min(tm, 256)),
        grid=(T // tm,),
        in_specs=[row(D), row(A_WIDTH), row(B_WIDTH), row(C_WIDTH), full(w), full(g2), full(wr_hl), full(wr_hi)],
        out_specs=[row(D), row(D // 2), pl.BlockSpec((N_EXPERTS, tm), lambda i: (0, i))],
        out_shape=[jax.ShapeDtypeStruct((T, D), F32), jax.ShapeDtypeStruct((T, D // 2), jnp.uint32),
                   jax.ShapeDtypeStruct((N_EXPERTS, T), F32)],
        compiler_params=_cparams(("parallel",)),
        name="outproj_router",
    )(x2d, oa, ob, oc, w, g2, wr_hl, wr_hi)


def _cumsum_tokens(x, upper, lstrict, ones):
    G, R, _ = x.shape
    xb = x.reshape(G * R, 128).astype(BF16)
    within = jnp.dot(xb, upper, preferred_element_type=F32).reshape(G, R, 128)
    rowtot = jnp.dot(xb, ones, preferred_element_type=F32).reshape(G, R, 128)
    hi = jnp.floor(rowtot * (1.0 / 256.0))
    lo = rowtot - 256.0 * hi
    outs = []
    for g in range(G):
        before = (256.0 * jnp.dot(lstrict, hi[g].astype(BF16), preferred_element_type=F32)
                  + jnp.dot(lstrict, lo[g].astype(BF16), preferred_element_type=F32))
        outs.append(within[g] + before)
    return jnp.stack(outs, axis=0)


def _route_kernel(aff_ref, sel_ref, pos_ref, dst_ref, off_ref, cnt_ref, *, cap):
    E, R, _ = aff_ref.shape
    bits = pltpu.bitcast(aff_ref[...].reshape(E * R, 128), jnp.int32).reshape(E, R, 128)

    def count_ge(v):
        return jnp.sum(jnp.sum((bits >= v).astype(jnp.int32), axis=1, keepdims=True), axis=2, keepdims=True)

    def bisect(i, prefix):
        cand = prefix | jnp.left_shift(jnp.int32(1), 30 - i)
        return jnp.where(count_ge(cand) >= cap, cand, prefix)

    thr = lax.fori_loop(0, 31, bisect, jnp.zeros((E, 1, 1), jnp.int32))
    gt = bits > thr
    eq = bits == thr
    n_gt = jnp.sum(jnp.sum(gt.astype(jnp.int32), axis=1, keepdims=True), axis=2, keepdims=True)
    need_eq = (cap - n_gt).astype(F32)

    r = lax.broadcasted_iota(jnp.int32, (128, 128), 0)
    cc = lax.broadcasted_iota(jnp.int32, (128, 128), 1)
    upper = (r <= cc).astype(BF16)
    ones = jnp.ones((128, 128), BF16)
    rr = lax.broadcasted_iota(jnp.int32, (R, R), 0)
    rc = lax.broadcasted_iota(jnp.int32, (R, R), 1)
    lstrict = (rc < rr).astype(BF16)

    eqf = eq.astype(F32)
    eq_rank = _cumsum_tokens(eqf, upper, lstrict, ones) - eqf
    sel = jnp.logical_or(gt, jnp.logical_and(eq, eq_rank < need_eq))
    self_ = sel.astype(F32)
    pos = _cumsum_tokens(self_, upper, lstrict, ones) - self_
    cnt = jnp.sum(self_, axis=0, keepdims=True)
    off = _cumsum_tokens(cnt, upper, lstrict, ones) - cnt
    rank = jnp.zeros((R, 128), F32)
    for e in range(E):
        dst_ref[e] = (off[0] + rank).astype(jnp.int32)
        rank = rank + self_[e]
    sel_ref[...] = sel.astype(jnp.int32)
    pos_ref[...] = pos.astype(jnp.int32)
    off_ref[...] = off[0].astype(jnp.int32)
    cnt_ref[...] = cnt[0].astype(jnp.int32)


def _route(aff_t, cap):
    E, T = aff_t.shape
    R = T // 128
    a3 = aff_t.reshape(E, R, 128)
    i3 = jax.ShapeDtypeStruct((E, R, 128), jnp.int32)
    i2 = jax.ShapeDtypeStruct((R, 128), jnp.int32)
    return pl.pallas_call(
        functools.partial(_route_kernel, cap=cap),
        out_shape=[i3, i3, i3, i2, i2],
        compiler_params=pltpu.CompilerParams(vmem_limit_bytes=VMEM_LIMIT),
        name="route_select",
    )(a3)


SC_CHUNK = 2048
SC_ROWS = 64


def _sc_mesh():
    return plsc.VectorSubcoreMesh(core_axis_name="c", subcore_axis_name="s")


def _sc_worker():
    info = pltpu.get_tpu_info().sparse_core
    return lax.axis_index("s") * info.num_cores + lax.axis_index("c"), info.num_cores * info.num_subcores


def _sc_compact(sel, pos, dst, aff, E, T, cap):
    @functools.partial(
        pl.kernel, mesh=_sc_mesh(),
        out_type=[jax.ShapeDtypeStruct((E * cap,), jnp.int32), jax.ShapeDtypeStruct((E * cap,), F32),
                  jax.ShapeDtypeStruct((E * cap,), jnp.int32)],
        scratch_types=[pltpu.VMEM((SC_CHUNK,), jnp.int32), pltpu.VMEM((SC_CHUNK,), jnp.int32),
                       pltpu.VMEM((SC_CHUNK,), jnp.int32), pltpu.VMEM((SC_CHUNK,), F32),
                       pltpu.VMEM((cap,), jnp.int32), pltpu.VMEM((cap,), F32), pltpu.VMEM((cap,), jnp.int32)],
        compiler_params=pltpu.CompilerParams(needs_layout_passes=False),
    )
    def k(sel_hbm, pos_hbm, dst_hbm, aff_hbm, idx_hbm, gate_hbm, dstl_hbm,
          sel_v, pos_v, dst_v, aff_v, idx_b, gate_b, dstl_b):
        e, _ = _sc_worker()

        @pl.when(e < E)
        def _():
            @pl.loop(0, T // SC_CHUNK)
            def _(ch):
                base = pl.multiple_of(e * T + ch * SC_CHUNK, 8)
                pltpu.sync_copy(sel_hbm.at[pl.ds(base, SC_CHUNK)], sel_v)
                pltpu.sync_copy(pos_hbm.at[pl.ds(base, SC_CHUNK)], pos_v)
                pltpu.sync_copy(dst_hbm.at[pl.ds(base, SC_CHUNK)], dst_v)
                pltpu.sync_copy(aff_hbm.at[pl.ds(base, SC_CHUNK)], aff_v)

                @pl.loop(0, SC_CHUNK // 16)
                def _(i):
                    sl = pl.ds(pl.multiple_of(i * 16, 16), 16)
                    chosen = sel_v[sl] > 0
                    slot = pos_v[sl]
                    tok = ch * SC_CHUNK + i * 16 + lax.iota(jnp.int32, 16)
                    plsc.store_scatter(idx_b, [slot], tok, mask=chosen)
                    plsc.store_scatter(gate_b, [slot], aff_v[sl], mask=chosen)
                    plsc.store_scatter(dstl_b, [slot], dst_v[sl], mask=chosen)

            out = pl.ds(pl.multiple_of(e * cap, 8), cap)
            pltpu.sync_copy(idx_b, idx_hbm.at[out])
            pltpu.sync_copy(gate_b, gate_hbm.at[out])
            pltpu.sync_copy(dstl_b, dstl_hbm.at[out])

    return k(sel, pos, dst, aff)


def _sc_gather_rows(table, idx):
    B, (_, D) = idx.shape[0], table.shape

    @functools.partial(
        pl.kernel, mesh=_sc_mesh(), out_type=jax.ShapeDtypeStruct((B, D), table.dtype),
        scratch_types=[pltpu.VMEM((SC_ROWS,), jnp.int32), pltpu.VMEM((SC_ROWS, D), table.dtype),
                       pltpu.SemaphoreType.DMA])
    def k(table_hbm, idx_hbm, out_hbm, idx_v, rows_v, sem):
        wid, nw = _sc_worker()
        per = B // nw

        @pl.loop(0, per // SC_ROWS)
        def _(g):
            off = pl.multiple_of(wid * per + g * SC_ROWS, 8)
            pltpu.sync_copy(idx_hbm.at[pl.ds(off, SC_ROWS)], idx_v)
            pltpu.async_copy(table_hbm.at[idx_v], rows_v, sem).wait()
            pltpu.sync_copy(rows_v, out_hbm.at[pl.ds(off, SC_ROWS)])

    return k(table, idx)


def _sc_scatter_rows(rows, dst):
    B, D = rows.shape

    @functools.partial(
        pl.kernel, mesh=_sc_mesh(), out_type=jax.ShapeDtypeStruct((B, D), rows.dtype),
        scratch_types=[pltpu.VMEM((SC_ROWS,), jnp.int32), pltpu.VMEM((SC_ROWS, D), rows.dtype),
                       pltpu.SemaphoreType.DMA])
    def k(rows_hbm, dst_hbm, out_hbm, dst_v, rows_v, sem):
        wid, nw = _sc_worker()
        per = B // nw

        @pl.loop(0, per // SC_ROWS)
        def _(g):
            off = pl.multiple_of(wid * per + g * SC_ROWS, 8)
            pltpu.sync_copy(dst_hbm.at[pl.ds(off, SC_ROWS)], dst_v)
            pltpu.sync_copy(rows_hbm.at[pl.ds(off, SC_ROWS)], rows_v)
            pltpu.async_copy(rows_v, out_hbm.at[dst_v], sem).wait()

    return k(rows, dst)


def _pack_bf16_pairs(x):
    m = x.shape[1] // 2
    bits = pltpu.bitcast(x.astype(BF16).astype(F32), jnp.uint32)
    return (bits[:, m:] & jnp.uint32(0xFFFF0000)) | (bits[:, :m] >> 16)


def _unpack_bf16_pairs(u):
    lo = pltpu.bitcast(u << 16, F32)
    hi = pltpu.bitcast(u & jnp.uint32(0xFFFF0000), F32)
    return jnp.concatenate([lo, hi], axis=1).astype(BF16)


def _combine_kernel(start_ref, h_ref, off_ref, cnt_ref, gf_ref, y_hbm, o_ref, ybuf, sem, acc_ref, nwin_done,
                    *, tb, win, nrows, final_norm):
    b = pl.program_id(0)
    nb = pl.num_programs(0)

    def window_start(blk, w):
        a8 = (start_ref[blk] // 8) * 8
        return pl.multiple_of(jnp.minimum(a8 + w * win, nrows - win), 8)

    def n_windows(blk):
        a8 = (start_ref[blk] // 8) * 8
        return jnp.maximum((start_ref[blk + 1] - a8 + win - 1) // win, 1)

    def copy(blk, w, slot):
        return pltpu.make_async_copy(y_hbm.at[pl.ds(window_start(blk, w), win)], ybuf.at[slot], sem.at[slot])

    def advance(blk, w):
        last = w + 1 >= n_windows(blk)
        return jnp.where(last, blk + 1, blk), jnp.where(last, 0, w + 1)

    nslot = ybuf.shape[0]

    @pl.when(b == 0)
    def _():
        nwin_done[0] = 0
        blk, w = jnp.int32(0), jnp.int32(0)
        for ahead in range(nslot - 1):
            @pl.when(blk < nb)
            def _():
                copy(blk, w, ahead).start()

            blk, w = advance(blk, w)

    rows = off_ref.shape[1]
    off_row = jnp.concatenate([off_ref[0, i:i + 1, :] for i in range(rows)], axis=1)
    end_row = off_row + jnp.concatenate([cnt_ref[0, i:i + 1, :] for i in range(rows)], axis=1)
    acc_ref[...] = jnp.zeros_like(acc_ref)
    nwin = n_windows(b)

    def body(w, carry):
        g = nwin_done[0]
        slot = g % nslot
        copy(b, w, slot).wait()
        blk, wn = b, w
        for _ in range(nslot - 1):
            blk, wn = advance(blk, wn)

        @pl.when(blk < nb)
        def _():
            copy(blk, wn, (g + nslot - 1) % nslot).start()

        first_new = (start_ref[b] // 8) * 8 + w * win
        r = window_start(b, w) + lax.broadcasted_iota(jnp.int32, (win, tb), 0)
        onehot_t = jnp.logical_and(jnp.logical_and(r >= off_row, r < end_row), r >= first_new)
        onehot_t = jnp.where(onehot_t, 1.0, 0.0).astype(BF16)
        y = _unpack_bf16_pairs(ybuf[slot])
        acc_ref[...] += lax.dot_general(onehot_t, y, (((0,), (0,)), ((), ())), preferred_element_type=F32)
        nwin_done[0] = g + 1
        return carry

    lax.fori_loop(0, nwin, body, 0)
    o = h_ref[...] + acc_ref[...]
    if final_norm:
        ms = jnp.mean(o * o, axis=-1, keepdims=True)
        o = o * lax.rsqrt(ms + EPS) * gf_ref[...]
    o_ref[...] = o


def _combine(h, off2d, cnt2d, blk_start, y_sorted, gf, tb, win, final_norm):
    T, D = h.shape
    nrows = y_sorted.shape[0]
    win = min(win, nrows)
    off3 = off2d.reshape(T // tb, tb // 128, 128)
    cnt3 = cnt2d.reshape(T // tb, tb // 128, 128)
    kern = functools.partial(_combine_kernel, tb=tb, win=win, nrows=nrows, final_norm=final_norm)
    grid_spec = pltpu.PrefetchScalarGridSpec(
        num_scalar_prefetch=1,
        grid=(T // tb,),
        in_specs=[pl.BlockSpec((tb, D), lambda i, s: (i, 0)),
                  pl.BlockSpec((1, tb // 128, 128), lambda i, s: (i, 0, 0)),
                  pl.BlockSpec((1, tb // 128, 128), lambda i, s: (i, 0, 0)),
                  pl.BlockSpec((1, D), lambda i, s: (0, 0)),
                  pl.BlockSpec(memory_space=pl.ANY)],
        out_specs=pl.BlockSpec((tb, D), lambda i, s: (i, 0)),
        scratch_shapes=[pltpu.VMEM((3, win, D // 2), jnp.uint32), pltpu.SemaphoreType.DMA((3,)),
                        pltpu.VMEM((tb, D), F32), pltpu.SMEM((1,), jnp.int32)],
    )
    return pl.pallas_call(
        kern, grid_spec=grid_spec, out_shape=jax.ShapeDtypeStruct((T, D), F32),
        compiler_params=_cparams(("arbitrary",)),
        name="combine",
    )(blk_start, h, off3, cnt3, gf, y_sorted)


def _ffn_kernel(x_ref, gate_ref, wg_ref, wu_ref, wd_ref, o_ref, *, fc):
    x = _unpack_bf16_pairs(x_ref[0])
    F = wg_ref.shape[2]
    acc = jnp.zeros((x.shape[0], wd_ref.shape[2]), F32)
    for c in range(F // fc):
        cols = slice(c * fc, (c + 1) * fc)
        g = jnp.dot(x, wg_ref[0, :, cols], preferred_element_type=F32)
        u = jnp.dot(x, wu_ref[0, :, cols], preferred_element_type=F32)
        hid = (g * jax.nn.sigmoid(g) * u).astype(BF16)
        acc = acc + jnp.dot(hid, wd_ref[0, cols, :], preferred_element_type=F32)
    o_ref[0] = _pack_bf16_pairs(acc * gate_ref[0])


def _ffn(xe, gate, wg, wu, wd, tm, fc):
    E, cap, Dh = xe.shape
    D = 2 * Dh
    F = wg.shape[2]
    fc = min(fc, F)
    return pl.pallas_call(
        functools.partial(_ffn_kernel, fc=fc),
        grid=(E, cap // tm),
        in_specs=[pl.BlockSpec((1, tm, Dh), lambda e, i: (e, i, 0)),
                  pl.BlockSpec((1, tm, 1), lambda e, i: (e, i, 0)),
                  pl.BlockSpec((1, D, F), lambda e, i: (e, 0, 0)),
                  pl.BlockSpec((1, D, F), lambda e, i: (e, 0, 0)),
                  pl.BlockSpec((1, F, D), lambda e, i: (e, 0, 0))],
        out_specs=pl.BlockSpec((1, tm, Dh), lambda e, i: (e, i, 0)),
        out_shape=jax.ShapeDtypeStruct((E, cap, Dh), jnp.uint32),
        compiler_params=_cparams(("parallel", "arbitrary")),
        name="expert_ffn",
    )(xe, gate, wg, wu, wd)


def _prep_layer(l, norm1, w_in, ln_v_g, ln_v_b, w_spatial, b_spatial, gla_decay_w, gla_decay_b, gla_norm,
                w_out, norm2, w_router, diff_subln):
    D = w_in.shape[1]
    w = jnp.pad(w_in[l], ((0, 0), (0, PROJ_PAD - PROJ_WIDTH))).astype(BF16)
    bs = jnp.repeat(b_spatial[l].T, B_GROUP_DIM, axis=1)
    wdec = jnp.zeros((128, 2 * C_KEY_WIDTH), F32)
    wdec = wdec.at[:C_DECAY_RANK, :C_KEY_WIDTH].set(gla_decay_w[l, 0])
    wdec = wdec.at[C_DECAY_RANK:2 * C_DECAY_RANK, C_KEY_WIDTH:].set(gla_decay_w[l, 1])
    bdec = gla_decay_b[l].reshape(1, 2 * C_KEY_WIDTH)
    wr = jnp.pad(w_router[l], ((0, 0), (0, 128 - N_EXPERTS)))
    wr_hi = wr.astype(BF16)
    wr_lo = (wr - wr_hi.astype(F32)).astype(BF16)
    return dict(
        g1=norm1[l].reshape(1, D), w=w, lng=ln_v_g[l].reshape(1, B_WIDTH), lnb=ln_v_b[l].reshape(1, B_WIDTH),
        ws=w_spatial[l].astype(BF16), bs=bs, wdec=wdec.astype(BF16), bdec=bdec,
        gn=jnp.tile(gla_norm[l], C_HEADS).reshape(1, C_WIDTH), w_out=w_out[l].astype(BF16),
        g2=norm2[l].reshape(1, D), wr_hi=wr_hi, wr_lo=wr_lo, subln=diff_subln[l].reshape(A_V_DIM, 1),
    )


def _tile(n, pref):
    t = min(n, pref)
    while n % t:
        t //= 2
    return t


def _trunk(x, layers, lam_vecs, ffn_w, norm_f):
    B, S, D = x.shape
    T = B * S
    cap = EC_CAPACITY_FACTOR * T // N_EXPERTS
    tm = _tile(T, 512)
    tk = _tile(S, 512)
    tq = min(tk, 256)
    kpos = _key_pos_lanes(tk)
    x2d = x.reshape(T, D)
    for l, p in enumerate(layers):
        lam_init = 0.8 - 0.6 * math.exp(-0.3 * l)
        aq, ak, avt, ob, cq, ck, cv, cg, laf, lab, stats = _inproj(
            x2d, p["g1"], p["w"], p["lng"], p["lnb"], p["ws"], p["bs"], p["wdec"], p["bdec"], tm, min(tk, tm), tq)
        r3 = lambda a: a.reshape(B, S, a.shape[-1])
        lq1, lk1, lq2, lk2 = (v[l:l + 1] for v in lam_vecs)
        oa = _attention(r3(aq), r3(ak), avt, stats, lq1, lk1, lq2, lk2, p["subln"], kpos, lam_init, tq, tk)
        oc = _gla(r3(cq), r3(ck), r3(cv), r3(cg), r3(laf), r3(lab), p["gn"], _tile(S, 512))
        h, hn, aff_t = _outproj(x2d, oa.reshape(T, A_WIDTH), ob, oc.reshape(T, C_WIDTH),
                                p["w_out"], p["g2"], p["wr_hi"], p["wr_lo"], tm)
        sel, pos, dst, off2d, cnt2d = _route(aff_t, cap)
        flat = lambda a: a.reshape(N_EXPERTS * T)
        idx, gate, dstl = _sc_compact(flat(sel), flat(pos), flat(dst), flat(aff_t), N_EXPERTS, T, cap)
        xe = _sc_gather_rows(hn, idx)
        wg, wu, wd = (w[l] for w in ffn_w)
        ye = _ffn(xe.reshape(N_EXPERTS, cap, D // 2), gate.reshape(N_EXPERTS, cap, 1), wg, wu, wd,
                  _tile(cap, 512), 512)
        y_sorted = _sc_scatter_rows(ye.reshape(N_EXPERTS * cap, D // 2), dstl)
        tb = _tile(T, 512)
        blk_start = jnp.concatenate([off2d.reshape(T)[::tb], jnp.full((4,), N_EXPERTS * cap, jnp.int32)])
        x2d = _combine(h, off2d, cnt2d, blk_start, y_sorted, norm_f.reshape(1, D), tb, 512,
                       final_norm=(l == len(layers) - 1))
    return x2d.reshape(B, S, D)


def kernel(x_prompt, x_sample, norm1, w_in, lam_q1, lam_k1, lam_q2, lam_k2, diff_subln, ln_v_g, ln_v_b,
           w_spatial, b_spatial, gla_decay_w, gla_decay_b, gla_norm, w_out, norm2, w_router, w_gate, w_up,
           w_down, norm_f):
    depth = w_in.shape[0]
    layers = [_prep_layer(l, norm1, w_in, ln_v_g, ln_v_b, w_spatial, b_spatial, gla_decay_w, gla_decay_b,
                          gla_norm, w_out, norm2, w_router, diff_subln) for l in range(depth)]
    ffn_w = (w_gate.astype(BF16), w_up.astype(BF16), w_down.astype(BF16))
    lam_vecs = (lam_q1, lam_k1, lam_q2, lam_k2)
    y_prompt = _trunk(x_prompt, layers, lam_vecs, ffn_w, norm_f)
    y_sample = _trunk(x_sample, layers, lam_vecs, ffn_w, norm_f)
    return (y_prompt, y_sample)
```

```python
import functools
import math

import jax
import jax.numpy as jnp
from jax import lax
from jax.experimental import pallas as pl
from jax.experimental.pallas import tpu as pltpu
from jax.experimental.pallas import tpu_sc as plsc

F32 = jnp.float32
BF16 = jnp.bfloat16

EPS = 1e-6
LOG2E = 1.4426950408889634

A_HEADS = 4
A_QK_DIM = 64
A_V_DIM = 128
A_WIDTH = A_HEADS * A_V_DIM
B_GROUPS = 4
B_WIDTH = 256
B_GROUP_DIM = 64
B_CHUNK = 128
C_HEADS = 4
C_WIDTH = 256
C_V_DIM = 64
C_K_DIM = 32
C_KEY_WIDTH = 128
C_DECAY_RANK = 16
C_GATE_NORMALIZER = 16.0
C_CHUNK = 64
N_EXPERTS = 16
EC_CAPACITY_FACTOR = 2

OFF_AQ, OFF_AK, OFF_AV = 0, 512, 1024
OFF_BU, OFF_BV = 1536, 1792
OFF_CQ, OFF_CK, OFF_CV, OFF_CG, OFF_CZ = 2048, 2176, 2304, 2560, 2816
PROJ_WIDTH = 2848
PROJ_PAD = 2944

VMEM_LIMIT = 56 * 1024 * 1024


def _cparams(sem):
    return pltpu.CompilerParams(dimension_semantics=sem, vmem_limit_bytes=VMEM_LIMIT)


def _gelu_tanh(x):
    return 0.5 * x * (1.0 + jnp.tanh(0.7978845608028654 * (x + 0.044715 * x * x * x)))


def _log_sigmoid(x):
    return jnp.minimum(x, 0.0) - jnp.log(1.0 + jnp.exp(-jnp.abs(x)))


def _inproj_kernel(x_ref, g1_ref, w_ref, lng_ref, lnb_ref, ws_ref, bs_ref, wdec_ref, bdec_ref,
                   aq_ref, ak_ref, av_ref, ob_ref, cq_ref, ck_ref, cv_ref, cg_ref, laf_ref, lab_ref,
                   st_ref, *, tq):
    x = x_ref[...]
    ms = jnp.mean(x * x, axis=-1, keepdims=True)
    hn = (x * lax.rsqrt(ms + EPS) * g1_ref[...]).astype(BF16)
    proj = jnp.dot(hn, w_ref[...], preferred_element_type=F32)

    aq_ref[...] = (proj[:, OFF_AQ:OFF_AK] * (A_QK_DIM ** -0.5 * LOG2E)).astype(BF16)
    ak_ref[...] = proj[:, OFF_AK:OFF_AV].astype(BF16)
    tkb = av_ref.shape[2]
    for c in range(av_ref.shape[0]):
        av_ref[c] = jnp.transpose(proj[c * tkb:(c + 1) * tkb, OFF_AV:OFF_BU]).astype(BF16)
    qs = proj[:, OFF_AQ:OFF_AK] * (A_QK_DIM ** -0.5 * LOG2E)
    kf = proj[:, OFF_AK:OFF_AV]
    grp = (lax.broadcasted_iota(jnp.int32, (A_WIDTH, 128), 0) // A_QK_DIM
           == lax.broadcasted_iota(jnp.int32, (A_WIDTH, 128), 1)).astype(BF16)
    gsum = lambda a: jnp.dot(a.astype(BF16), grp, preferred_element_type=F32)
    qn2, kn2, ss = gsum(qs * qs), gsum(kf * kf), gsum(qs * kf)
    ntile = tkb // tq
    for c in range(av_ref.shape[0]):
        rows = [jnp.max(kn2[c * tkb:(c + 1) * tkb], axis=0, keepdims=True)]
        for t in range(ntile):
            lo = c * tkb + t * tq
            rows.append(jnp.max(qn2[lo:lo + tq], axis=0, keepdims=True))
        for t in range(ntile):
            lo = c * tkb + t * tq
            rows.append(jnp.min(ss[lo:lo + tq], axis=0, keepdims=True))
        rows += [jnp.zeros((1, 128), F32)] * (8 - len(rows))
        st_ref[c] = jnp.concatenate(rows, axis=0)

    u = _gelu_tanh(proj[:, OFF_BU:OFF_BV])
    v = _gelu_tanh(proj[:, OFF_BV:OFF_CQ])
    mu = jnp.mean(v, axis=-1, keepdims=True)
    var = jnp.mean(jnp.square(v - mu), axis=-1, keepdims=True)
    v = ((v - mu) * lax.rsqrt(var + EPS) * lng_ref[...] + lnb_ref[...]).astype(BF16)
    tm = x.shape[0]
    lane = lax.broadcasted_iota(jnp.int32, (B_CHUNK, 128), 1)
    first_half = lane < B_GROUP_DIM
    for c in range(tm // B_CHUNK):
        rows = slice(c * B_CHUNK, (c + 1) * B_CHUNK)
        parts = []
        for p in range(B_GROUPS // 2):
            vch = v[rows, p * 128:(p + 1) * 128]
            m0 = jnp.dot(ws_ref[2 * p], vch, preferred_element_type=F32)
            m1 = jnp.dot(ws_ref[2 * p + 1], vch, preferred_element_type=F32)
            parts.append(jnp.where(first_half, m0, m1))
        mixed = jnp.concatenate(parts, axis=1) + bs_ref[...]
        ob_ref[rows, :] = (u[rows, :] * mixed).astype(BF16)

    cq_ref[...] = proj[:, OFF_CQ:OFF_CK] * (C_K_DIM ** -0.5)
    ck_ref[...] = proj[:, OFF_CK:OFF_CV]
    cv_ref[...] = proj[:, OFF_CV:OFF_CG].astype(BF16)
    cg_ref[...] = proj[:, OFF_CG:OFF_CZ]
    z = proj[:, OFF_CZ:PROJ_PAD].astype(BF16)
    xd = jnp.dot(z, wdec_ref[...], preferred_element_type=F32) + bdec_ref[...]
    la = _log_sigmoid(xd) * (1.0 / C_GATE_NORMALIZER)
    laf_ref[...] = la[:, :C_KEY_WIDTH]
    lab_ref[...] = la[:, C_KEY_WIDTH:]


def _inproj(x2d, g1, w, lng, lnb, ws, bs, wdec, bdec, tm, tk, tq):
    T, D = x2d.shape
    row = lambda n: pl.BlockSpec((tm, n), lambda i: (i, 0))
    full = lambda a: pl.BlockSpec(a.shape, lambda i: (0,) * a.ndim)
    outs = [
        (A_WIDTH, BF16), (A_WIDTH, BF16), None, (B_WIDTH, BF16),
        (C_KEY_WIDTH, F32), (C_KEY_WIDTH, F32), (C_WIDTH, BF16), (C_WIDTH, F32),
        (C_KEY_WIDTH, F32), (C_KEY_WIDTH, F32),
    ]
    vt_spec = pl.BlockSpec((tm // tk, A_WIDTH, tk), lambda i: (i, 0, 0))
    vt_shape = jax.ShapeDtypeStruct((T // tk, A_WIDTH, tk), BF16)
    st_spec = pl.BlockSpec((tm // tk, 8, 128), lambda i: (i, 0, 0))
    st_shape = jax.ShapeDtypeStruct((T // tk, 8, 128), F32)
    return pl.pallas_call(
        functools.partial(_inproj_kernel, tq=tq),
        grid=(T // tm,),
        in_specs=[row(D), full(g1), full(w), full(lng), full(lnb), full(ws), full(bs), full(wdec), full(bdec)],
        out_specs=[vt_spec if o is None else row(o[0]) for o in outs] + [st_spec],
        out_shape=[vt_shape if o is None else jax.ShapeDtypeStruct((T, o[0]), o[1]) for o in outs] + [st_shape],
        compiler_params=_cparams(("parallel",)),
        name="inproj",
    )(x2d, g1, w, lng, lnb, ws, bs, wdec, bdec)


N_POS_LANES = 9


def _split3(x):
    hi = x.astype(BF16)
    r1 = x - hi.astype(F32)
    mid = r1.astype(BF16)
    lo = (r1 - mid.astype(F32)).astype(BF16)
    return hi, mid, lo


def _attn_kernel(lq1_ref, lk1_ref, lq2_ref, lk2_ref, g_ref, kpos_ref, st_ref, q_ref, k_ref, vt_ref, o_ref,
                 qall_ref, *chain_refs, tq, tk, nq, lam_init):
    nch = 2 * nq
    grp = lambda g: chain_refs[g * nch:(g + 1) * nch]
    m_refs, l_refs, acc_refs = grp(0), grp(1), grp(2)
    s_refs = (grp(3), grp(4))
    h = pl.program_id(1)
    qi = pl.program_id(2)
    nblk = k_ref.shape[1] // tk
    lam = (jnp.exp(jnp.sum(lq1_ref[...] * lk1_ref[...], axis=-1, keepdims=True))
           - jnp.exp(jnp.sum(lq2_ref[...] * lk2_ref[...], axis=-1, keepdims=True)) + lam_init)
    c = jnp.exp2(-8.0 * (h + 1).astype(F32) / A_HEADS) * LOG2E

    lane = lax.broadcasted_iota(jnp.int32, (tq, 128), 1)
    il = lax.broadcasted_iota(jnp.int32, (tq, 128), 0).astype(F32)
    hi, mid, lo = _split3(jnp.where(lane < 3, -c * il, c))
    piece = lane % 3
    qpos = jnp.where(piece == 0, hi, jnp.where(piece == 1, mid, lo))
    qpos = jnp.where(lane < N_POS_LANES, qpos, jnp.zeros_like(qpos))
    chains = [(t, m) for t in range(nq) for m in range(2)]
    for ci, (t, m) in enumerate(chains):
        q = q_ref[0, t * tq:(t + 1) * tq, :]
        keep = (lane < A_QK_DIM) if m == 0 else (lane >= A_QK_DIM)
        qm = jnp.where(keep, q, jnp.zeros_like(q))
        qall_ref[0, ci] = jnp.concatenate([qm, qpos], axis=1)
        qall_ref[1, ci] = jnp.concatenate([qm, -qpos], axis=1)
    kpos = kpos_ref[...]

    def block_of(i):
        above = (i >= qi).astype(jnp.int32)
        return i + above, above

    def scores(j, variant, ci):
        kaug = jnp.concatenate([k_ref[0, pl.ds(pl.multiple_of(j * tk, tk), tk), :], kpos], axis=1)
        return lax.dot_general(kaug, qall_ref[variant, ci], (((1,), (1,)), ((), ())),
                               preferred_element_type=F32)

    def softmax(j, ci, s, first):
        t, _ = chains[ci]
        off = ((qi * nq + t) * tq - j * tk).astype(F32)
        delta = c * jnp.abs(off)
        if first:
            rel = (lax.broadcasted_iota(jnp.int32, (tk, tq), 0)
                   - lax.broadcasted_iota(jnp.int32, (tk, tq), 1)).astype(F32)
            s = s - (2.0 * c) * jnp.maximum(rel - float(t * tq), 0.0)
        bmax = jnp.max(s, axis=0, keepdims=True) - delta
        if first:
            mn = bmax
        else:
            mx = m_refs[ci][...]
            mn = jnp.maximum(mx, bmax)
        p = jnp.exp2(s - (mn + delta))
        psum = jnp.sum(p, axis=0, keepdims=True)
        m_refs[ci][...] = mn
        if first:
            l_refs[ci][...] = psum
            return p.astype(BF16), None
        alpha = jnp.exp2(mx - mn)
        l_refs[ci][...] = alpha * l_refs[ci][...] + psum
        return p.astype(BF16), alpha

    n = nblk
    if n >= 4:
        st = st_ref[...]
        mine = st_ref[qi]
        jv = lax.broadcasted_iota(jnp.int32, (n, 128), 0)
        gl = lax.broadcasted_iota(jnp.int32, (n, 128), 1)
        ub = jnp.full((n, 128), -jnp.inf, F32)
        for t in range(nq):
            i0 = (qi * nq + t) * tq
            gap = jnp.where(jv < qi, i0 - (jv * tk + tk - 1), jv * tk - (i0 + tq - 1))
            dmin = jnp.maximum(gap, 0).astype(F32)
            bound = 1.1 * jnp.sqrt(mine[1 + t:2 + t, :] * st[:, 0, :]) - mine[1 + nq + t:2 + nq + t, :] - c * dmin
            ub = jnp.maximum(ub, bound)
        ub = jnp.where(gl // 2 == h, ub, -jnp.inf)
        live = jnp.logical_or(jnp.max(ub, axis=1, keepdims=True) > -150.0, jv[:, :1] == qi)
        j1 = jv[:, :1]
        jlo = jnp.min(jnp.where(live, j1, n))
        jhi = jnp.max(jnp.where(live, j1, -1))
        cnt = jhi - jlo
        need = cnt + 1 - cnt % 2
        grow_hi = jnp.minimum(need - cnt, (n - 1) - jhi)
        lo = jlo - (need - cnt - grow_hi)
        nvis = need + 1
    else:
        lo = 0
        nvis = n

    def visit(u):
        if isinstance(u, int) and u == 0:
            return qi, 0
        return block_of(lo + u - 1)

    def tick(tau, do_qk, do_sm):
        tau, par = tau
        first = isinstance(tau, int) and tau == 1
        if do_qk:
            jq, vq = visit(tau)
        if do_sm:
            js, _ = visit(tau - 1)
        for half in range(0, nch, 2):
            for ci in (half, half + 1):
                if do_qk:
                    s_refs[par][ci][...] = scores(jq, vq, ci)
            for ci in (half, half + 1):
                if do_sm:
                    p, alpha = softmax(js, ci, s_refs[1 - par][ci][...], first)
                    pv = jnp.dot(vt_ref[js], p, preferred_element_type=F32)
                    if first:
                        acc_refs[ci][...] = pv
                    else:
                        acc_refs[ci][...] = alpha * acc_refs[ci][...] + pv

    assert n == 1 or n % 2 == 0
    tick((0, 0), True, False)
    tick((1, 1), n > 1, True)
    if n > 1:
        def pair(ip, carry):
            tau = 2 + 2 * ip
            tick((tau, 0), True, True)
            tick((tau + 1, 1), True, True)
            return carry

        lax.fori_loop(0, (nvis - 2) // 2, pair, 0)
        tick((nvis, 0), False, True)

    for t in range(nq):
        a0, a1 = acc_refs[2 * t][...], acc_refs[2 * t + 1][...]
        o = a0 / l_refs[2 * t][...] - lam * (a1 / l_refs[2 * t + 1][...])
        ms = jnp.mean(o * o, axis=0, keepdims=True)
        o = o * lax.rsqrt(ms + EPS) * g_ref[...] * (1.0 - lam_init)
        o_ref[0, t * tq:(t + 1) * tq, :] = jnp.transpose(o).astype(o_ref.dtype)


def _attention(aq, ak, avt, stats, lq1, lk1, lq2, lk2, subln_col, kpos, lam_init, tq, tk):
    B, S, _ = aq.shape
    nq = tk // tq
    nch = 2 * nq
    assert S == tk or (S // tk) % 2 == 0
    vec = pl.BlockSpec((1, A_QK_DIM), lambda b, h, i: (0, 0))
    kern = functools.partial(_attn_kernel, tq=tq, tk=tk, nq=nq, lam_init=lam_init)
    return pl.pallas_call(
        kern,
        grid=(B, A_HEADS, S // tk),
        in_specs=[vec, vec, vec, vec,
                  pl.BlockSpec((A_V_DIM, 1), lambda b, h, i: (0, 0)),
                  pl.BlockSpec((tk, 128), lambda b, h, i: (0, 0)),
                  pl.BlockSpec((S // tk, 8, 128), lambda b, h, i: (b, 0, 0)),
                  pl.BlockSpec((1, tk, 128), lambda b, h, i: (b, i, h)),
                  pl.BlockSpec((1, S, 128), lambda b, h, i: (b, 0, h)),
                  pl.BlockSpec((S // tk, A_V_DIM, tk), lambda b, h, i: (b, h, 0))],
        out_specs=pl.BlockSpec((1, tk, 128), lambda b, h, i: (b, i, h)),
        out_shape=jax.ShapeDtypeStruct((B, S, A_WIDTH), BF16),
        scratch_shapes=([pltpu.VMEM((2, nch, tq, 256), BF16)]
                        + [pltpu.VMEM((1, tq), F32)] * (2 * nch)
                        + [pltpu.VMEM((A_V_DIM, tq), F32)] * nch
                        + [pltpu.VMEM((tk, tq), F32)] * (2 * nch)),
        compiler_params=_cparams(("parallel", "parallel", "arbitrary")),
        name="diff_attn",
    )(lq1, lk1, lq2, lk2, subln_col, kpos, stats, aq, ak, avt)


def _key_pos_lanes(tk):
    j = jnp.arange(tk, dtype=jnp.int32)
    jlo = (j % 256).astype(F32)
    jhi = (j - j % 256).astype(F32)
    cols = [jnp.ones((tk,), F32)] * 3 + [jlo] * 3 + [jhi] * 3
    kp = jnp.stack(cols, axis=1)
    return jnp.pad(kp, ((0, 0), (0, 128 - N_POS_LANES))).astype(BF16)


def _gla_chunk(q, k, v, la, st, reverse):
    C = C_CHUNK
    r = lax.broadcasted_iota(jnp.int32, (C, C), 0)
    c = lax.broadcasted_iota(jnp.int32, (C, C), 1)
    tri = (c >= r) if reverse else (c <= r)
    tri_b = tri.astype(BF16)
    la_hi = la.astype(BF16)
    la_lo = (la - la_hi.astype(F32)).astype(BF16)
    b = (jnp.dot(tri_b, la_hi, preferred_element_type=F32)
         + jnp.dot(tri_b, la_lo, preferred_element_type=F32))
    b_end = b[0:1, :] if reverse else b[C - 1:C, :]
    qt = (q * jnp.exp(b)).astype(BF16)
    b_mid = b[C // 2:C // 2 + 1, :]
    qa = (q * jnp.exp(b - b_mid)).astype(BF16)
    kt = k * jnp.exp(b_mid - b)
    kdec = (k * jnp.exp(b_end - b)).astype(BF16)

    lane_k = lax.broadcasted_iota(jnp.int32, (C, C_KEY_WIDTH), 1) // C_K_DIM
    kstack = jnp.concatenate([jnp.where(lane_k == hh, kt, 0.0) for hh in range(C_HEADS)], axis=0).astype(BF16)
    attn = lax.dot_general(qa, kstack, (((1,), (1,)), ((), ())), preferred_element_type=F32)
    ri = lax.broadcasted_iota(jnp.int32, (C, C_HEADS * C), 0)
    cj = lax.broadcasted_iota(jnp.int32, (C, C_HEADS * C), 1) % C
    keep = (cj >= ri) if reverse else (cj <= ri)
    attn = jnp.where(keep, attn, 0.0).astype(BF16)
    lane_v = lax.broadcasted_iota(jnp.int32, (C, C_WIDTH), 1) // C_V_DIM
    vstack = jnp.concatenate([jnp.where(lane_v == hh, v, jnp.zeros_like(v)) for hh in range(C_HEADS)], axis=0)
    o = jnp.dot(attn, vstack, preferred_element_type=F32)
    o = o + lax.dot_general(qt, st.astype(BF16), (((1,), (1,)), ((), ())), preferred_element_type=F32)

    upd = lax.dot_general(v, kdec, (((0,), (0,)), ((), ())), preferred_element_type=F32)
    rh = lax.broadcasted_iota(jnp.int32, (C_WIDTH, C_KEY_WIDTH), 0) // C_V_DIM
    ch = lax.broadcasted_iota(jnp.int32, (C_WIDTH, C_KEY_WIDTH), 1) // C_K_DIM
    st = st * jnp.exp(b_end) + jnp.where(rh == ch, upd, 0.0)
    return o, st


def _gla_fwd_kernel(q_ref, k_ref, v_ref, la_ref, o_ref, st_ref, *, nchunk):
    @pl.when(pl.program_id(1) == 0)
    def _():
        st_ref[...] = jnp.zeros_like(st_ref)

    G = q_ref.shape[0]
    st = [st_ref[g] for g in range(G)]
    for c in range(nchunk):
        rows = slice(c * C_CHUNK, (c + 1) * C_CHUNK)
        for g in range(G):
            o, st[g] = _gla_chunk(q_ref[g, rows, :], k_ref[g, rows, :], v_ref[g, rows, :], la_ref[g, rows, :],
                                  st[g], False)
            o_ref[g, rows, :] = o
    for g in range(G):
        st_ref[g] = st[g]


def _gla_bwd_kernel(q_ref, k_ref, v_ref, la_ref, of_ref, g_ref, gn_ref, o_ref, st_ref, *, nchunk):
    @pl.when(pl.program_id(1) == 0)
    def _():
        st_ref[...] = jnp.zeros_like(st_ref)

    G = q_ref.shape[0]
    st = [st_ref[g] for g in range(G)]
    lane_h = lax.broadcasted_iota(jnp.int32, (C_WIDTH, C_WIDTH), 0) // C_V_DIM
    lane_h2 = lax.broadcasted_iota(jnp.int32, (C_WIDTH, C_WIDTH), 1) // C_V_DIM
    segb = ((lane_h == lane_h2).astype(F32) * (1.0 / C_V_DIM)).astype(BF16)
    for c in reversed(range(nchunk)):
        rows = slice(c * C_CHUNK, (c + 1) * C_CHUNK)
        for g in range(G):
            o, st[g] = _gla_chunk(q_ref[g, rows, :], k_ref[g, rows, :], v_ref[g, rows, :], la_ref[g, rows, :],
                                  st[g], True)
            o = o + of_ref[g, rows, :]
            sq = o * o
            sq_hi = sq.astype(BF16)
            sq_lo = (sq - sq_hi.astype(F32)).astype(BF16)
            ms = (jnp.dot(sq_hi, segb, preferred_element_type=F32)
                  + jnp.dot(sq_lo, segb, preferred_element_type=F32))
            o = o * lax.rsqrt(ms + EPS) * gn_ref[...]
            gate = g_ref[g, rows, :]
            o_ref[g, rows, :] = (o * (gate * jax.nn.sigmoid(gate))).astype(o_ref.dtype)
    for g in range(G):
        st_ref[g] = st[g]


def _gla(cq, ck, cv, cg, laf, lab, gn_tiled, rb):
    B, S, _ = cq.shape
    nb = S // rb
    nchunk = rb // C_CHUNK
    G = 2 if B % 2 == 0 else 1
    fspec = lambda n: pl.BlockSpec((G, rb, n), lambda b, i: (b, i, 0))
    bspec = lambda n: pl.BlockSpec((G, rb, n), lambda b, i: (b, nb - 1 - i, 0))
    st = pltpu.VMEM((G, C_WIDTH, C_KEY_WIDTH), F32)
    o_f = pl.pallas_call(
        functools.partial(_gla_fwd_kernel, nchunk=nchunk),
        grid=(B // G, nb),
        in_specs=[fspec(C_KEY_WIDTH), fspec(C_KEY_WIDTH), fspec(C_WIDTH), fspec(C_KEY_WIDTH)],
        out_specs=fspec(C_WIDTH),
        out_shape=jax.ShapeDtypeStruct((B, S, C_WIDTH), F32),
        scratch_shapes=[st],
        compiler_params=_cparams(("parallel", "arbitrary")),
        name="gla_fwd",
    )(cq, ck, cv, laf)
    return pl.pallas_call(
        functools.partial(_gla_bwd_kernel, nchunk=nchunk),
        grid=(B // G, nb),
        in_specs=[bspec(C_KEY_WIDTH), bspec(C_KEY_WIDTH), bspec(C_WIDTH), bspec(C_KEY_WIDTH),
                  bspec(C_WIDTH), bspec(C_WIDTH), pl.BlockSpec((1, C_WIDTH), lambda b, i: (0, 0))],
        out_specs=bspec(C_WIDTH),
        out_shape=jax.ShapeDtypeStruct((B, S, C_WIDTH), BF16),
        scratch_shapes=[st],
        compiler_params=_cparams(("parallel", "arbitrary")),
        name="gla_bwd",
    )(cq, ck, cv, lab, o_f, cg, gn_tiled)


def _outproj_kernel(x_ref, oa_ref, ob_ref, oc_ref, w_ref, g2_ref, wr_hl_ref, wr_hi_ref,
                    h_ref, hn_ref, aff_ref, *, sub):
    for r in range(x_ref.shape[0] // sub):
        rows = slice(r * sub, (r + 1) * sub)
        mix = jnp.concatenate([oa_ref[rows, :], ob_ref[rows, :], oc_ref[rows, :]], axis=1)
        h = x_ref[rows, :] + jnp.dot(mix, w_ref[...], preferred_element_type=F32)
        h_ref[rows, :] = h
        ms = jnp.mean(h * h, axis=-1, keepdims=True)
        hn = h * lax.rsqrt(ms + EPS) * g2_ref[...]
        hn_ref[rows, :] = _pack_bf16_pairs(hn)
        hn_hi = hn.astype(BF16)
        hn_lo = (hn - hn_hi.astype(F32)).astype(BF16)
        both = jnp.dot(hn_hi, wr_hl_ref[...], preferred_element_type=F32)
        logits = both[:, :128] + both[:, 128:] + jnp.dot(hn_lo, wr_hi_ref[...], preferred_element_type=F32)
        lane = lax.broadcasted_iota(jnp.int32, logits.shape, 1)
        logits = jnp.where(lane < N_EXPERTS, logits, -jnp.inf)
        mx = jnp.max(logits, axis=-1, keepdims=True)
        e = jnp.exp(logits - mx)
        aff = e / jnp.sum(e, axis=-1, keepdims=True)
        aff_ref[:, rows] = jnp.transpose(aff)[:N_EXPERTS, :]


def _outproj(x2d, oa, ob, oc, w, g2, wr_hi, wr_lo, tm):
    T, D = x2d.shape
    wr_hl = jnp.concatenate([wr_hi, wr_lo], axis=1)
    row = lambda n: pl.BlockSpec((tm, n), lambda i: (i, 0))
    full = lambda a: pl.BlockSpec(a.shape, lambda i: (0,) * a.ndim)
    return pl.pallas_call(
        functools.partial(_outproj_kernel, sub=min(tm, 256)),
        grid=(T // tm,),
        in_specs=[row(D), row(A_WIDTH), row(B_WIDTH), row(C_WIDTH), full(w), full(g2), full(wr_hl), full(wr_hi)],
        out_specs=[row(D), row(D // 2), pl.BlockSpec((N_EXPERTS, tm), lambda i: (0, i))],
        out_shape=[jax.ShapeDtypeStruct((T, D), F32), jax.ShapeDtypeStruct((T, D // 2), jnp.uint32),
                   jax.ShapeDtypeStruct((N_EXPERTS, T), F32)],
        compiler_params=_cparams(("parallel",)),
        name="outproj_router",
    )(x2d, oa, ob, oc, w, g2, wr_hl, wr_hi)


def _cumsum_tokens(x, upper, lstrict, ones):
    G, R, _ = x.shape
    xb = x.reshape(G * R, 128).astype(BF16)
    within = jnp.dot(xb, upper, preferred_element_type=F32).reshape(G, R, 128)
    rowtot = jnp.dot(xb, ones, preferred_element_type=F32).reshape(G, R, 128)
    hi = jnp.floor(rowtot * (1.0 / 256.0))
    lo = rowtot - 256.0 * hi
    outs = []
    for g in range(G):
        before = (256.0 * jnp.dot(lstrict, hi[g].astype(BF16), preferred_element_type=F32)
                  + jnp.dot(lstrict, lo[g].astype(BF16), preferred_element_type=F32))
        outs.append(within[g] + before)
    return jnp.stack(outs, axis=0)


def _route_kernel(aff_ref, sel_ref, pos_ref, dst_ref, off_ref, cnt_ref, *, cap):
    E, R, _ = aff_ref.shape
    bits = pltpu.bitcast(aff_ref[...].reshape(E * R, 128), jnp.int32).reshape(E, R, 128)

    def count_ge(v):
        return jnp.sum(jnp.sum((bits >= v).astype(jnp.int32), axis=1, keepdims=True), axis=2, keepdims=True)

    def bisect(i, prefix):
        cand = prefix | jnp.left_shift(jnp.int32(1), 30 - i)
        return jnp.where(count_ge(cand) >= cap, cand, prefix)

    thr = lax.fori_loop(0, 31, bisect, jnp.zeros((E, 1, 1), jnp.int32))
    gt = bits > thr
    eq = bits == thr
    n_gt = jnp.sum(jnp.sum(gt.astype(jnp.int32), axis=1, keepdims=True), axis=2, keepdims=True)
    need_eq = (cap - n_gt).astype(F32)

    r = lax.broadcasted_iota(jnp.int32, (128, 128), 0)
    cc = lax.broadcasted_iota(jnp.int32, (128, 128), 1)
    upper = (r <= cc).astype(BF16)
    ones = jnp.ones((128, 128), BF16)
    rr = lax.broadcasted_iota(jnp.int32, (R, R), 0)
    rc = lax.broadcasted_iota(jnp.int32, (R, R), 1)
    lstrict = (rc < rr).astype(BF16)

    eqf = eq.astype(F32)
    eq_rank = _cumsum_tokens(eqf, upper, lstrict, ones) - eqf
    sel = jnp.logical_or(gt, jnp.logical_and(eq, eq_rank < need_eq))
    self_ = sel.astype(F32)
    pos = _cumsum_tokens(self_, upper, lstrict, ones) - self_
    cnt = jnp.sum(self_, axis=0, keepdims=True)
    off = _cumsum_tokens(cnt, upper, lstrict, ones) - cnt
    rank = jnp.zeros((R, 128), F32)
    for e in range(E):
        dst_ref[e] = (off[0] + rank).astype(jnp.int32)
        rank = rank + self_[e]
    sel_ref[...] = sel.astype(jnp.int32)
    pos_ref[...] = pos.astype(jnp.int32)
    off_ref[...] = off[0].astype(jnp.int32)
    cnt_ref[...] = cnt[0].astype(jnp.int32)


def _route(aff_t, cap):
    E, T = aff_t.shape
    R = T // 128
    a3 = aff_t.reshape(E, R, 128)
    i3 = jax.ShapeDtypeStruct((E, R, 128), jnp.int32)
    i2 = jax.ShapeDtypeStruct((R, 128), jnp.int32)
    return pl.pallas_call(
        functools.partial(_route_kernel, cap=cap),
        out_shape=[i3, i3, i3, i2, i2],
        compiler_params=pltpu.CompilerParams(vmem_limit_bytes=VMEM_LIMIT),
        name="route_select",
    )(a3)


SC_CHUNK = 2048
SC_ROWS = 128


def _sc_mesh():
    return plsc.VectorSubcoreMesh(core_axis_name="c", subcore_axis_name="s")


def _sc_worker():
    info = pltpu.get_tpu_info().sparse_core
    return lax.axis_index("s") * info.num_cores + lax.axis_index("c"), info.num_cores * info.num_subcores


def _sc_compact(sel, pos, dst, aff, E, T, cap):
    @functools.partial(
        pl.kernel, mesh=_sc_mesh(),
        out_type=[jax.ShapeDtypeStruct((E * cap,), jnp.int32), jax.ShapeDtypeStruct((E * cap,), F32),
                  jax.ShapeDtypeStruct((E * cap,), jnp.int32)],
        scratch_types=[pltpu.VMEM((SC_CHUNK,), jnp.int32), pltpu.VMEM((SC_CHUNK,), jnp.int32),
                       pltpu.VMEM((SC_CHUNK,), jnp.int32), pltpu.VMEM((SC_CHUNK,), F32),
                       pltpu.VMEM((cap,), jnp.int32), pltpu.VMEM((cap,), F32), pltpu.VMEM((cap,), jnp.int32)],
        compiler_params=pltpu.CompilerParams(needs_layout_passes=False),
    )
    def k(sel_hbm, pos_hbm, dst_hbm, aff_hbm, idx_hbm, gate_hbm, dstl_hbm,
          sel_v, pos_v, dst_v, aff_v, idx_b, gate_b, dstl_b):
        e, _ = _sc_worker()

        @pl.when(e < E)
        def _():
            @pl.loop(0, T // SC_CHUNK)
            def _(ch):
                base = pl.multiple_of(e * T + ch * SC_CHUNK, 8)
                pltpu.sync_copy(sel_hbm.at[pl.ds(base, SC_CHUNK)], sel_v)
                pltpu.sync_copy(pos_hbm.at[pl.ds(base, SC_CHUNK)], pos_v)
                pltpu.sync_copy(dst_hbm.at[pl.ds(base, SC_CHUNK)], dst_v)
                pltpu.sync_copy(aff_hbm.at[pl.ds(base, SC_CHUNK)], aff_v)

                @pl.loop(0, SC_CHUNK // 16)
                def _(i):
                    sl = pl.ds(pl.multiple_of(i * 16, 16), 16)
                    chosen = sel_v[sl] > 0
                    slot = pos_v[sl]
                    tok = ch * SC_CHUNK + i * 16 + lax.iota(jnp.int32, 16)
                    plsc.store_scatter(idx_b, [slot], tok, mask=chosen)
                    plsc.store_scatter(gate_b, [slot], aff_v[sl], mask=chosen)
                    plsc.store_scatter(dstl_b, [slot], dst_v[sl], mask=chosen)

            out = pl.ds(pl.multiple_of(e * cap, 8), cap)
            pltpu.sync_copy(idx_b, idx_hbm.at[out])
            pltpu.sync_copy(gate_b, gate_hbm.at[out])
            pltpu.sync_copy(dstl_b, dstl_hbm.at[out])

    return k(sel, pos, dst, aff)


def _sc_gather_rows(table, idx):
    B, (_, D) = idx.shape[0], table.shape

    @functools.partial(
        pl.kernel, mesh=_sc_mesh(), out_type=jax.ShapeDtypeStruct((B, D), table.dtype),
        scratch_types=[pltpu.VMEM((SC_ROWS,), jnp.int32), pltpu.VMEM((SC_ROWS, D), table.dtype),
                       pltpu.SemaphoreType.DMA])
    def k(table_hbm, idx_hbm, out_hbm, idx_v, rows_v, sem):
        wid, nw = _sc_worker()
        per = B // nw

        @pl.loop(0, per // SC_ROWS)
        def _(g):
            off = pl.multiple_of(wid * per + g * SC_ROWS, 8)
            pltpu.sync_copy(idx_hbm.at[pl.ds(off, SC_ROWS)], idx_v)
            pltpu.async_copy(table_hbm.at[idx_v], rows_v, sem).wait()
            pltpu.sync_copy(rows_v, out_hbm.at[pl.ds(off, SC_ROWS)])

    return k(table, idx)


def _sc_scatter_rows(rows, dst):
    B, D = rows.shape

    @functools.partial(
        pl.kernel, mesh=_sc_mesh(), out_type=jax.ShapeDtypeStruct((B, D), rows.dtype),
        scratch_types=[pltpu.VMEM((SC_ROWS,), jnp.int32), pltpu.VMEM((SC_ROWS, D), rows.dtype),
                       pltpu.SemaphoreType.DMA])
    def k(rows_hbm, dst_hbm, out_hbm, dst_v, rows_v, sem):
        wid, nw = _sc_worker()
        per = B // nw

        @pl.loop(0, per // SC_ROWS)
        def _(g):
            off = pl.multiple_of(wid * per + g * SC_ROWS, 8)
            pltpu.sync_copy(dst_hbm.at[pl.ds(off, SC_ROWS)], dst_v)
            pltpu.sync_copy(rows_hbm.at[pl.ds(off, SC_ROWS)], rows_v)
            pltpu.async_copy(rows_v, out_hbm.at[dst_v], sem).wait()

    return k(rows, dst)


def _pack_bf16_pairs(x):
    m = x.shape[1] // 2
    bits = pltpu.bitcast(x.astype(BF16).astype(F32), jnp.uint32)
    return (bits[:, m:] & jnp.uint32(0xFFFF0000)) | (bits[:, :m] >> 16)


def _unpack_bf16_pairs(u):
    lo = pltpu.bitcast(u << 16, F32)
    hi = pltpu.bitcast(u & jnp.uint32(0xFFFF0000), F32)
    return jnp.concatenate([lo, hi], axis=1).astype(BF16)


def _combine_kernel(start_ref, h_ref, off_ref, cnt_ref, gf_ref, y_hbm, o_ref, ybuf, sem, acc_ref, nwin_done,
                    *, tb, win, nrows, final_norm):
    b = pl.program_id(0)
    nb = pl.num_programs(0)

    def window_start(blk, w):
        a8 = (start_ref[blk] // 8) * 8
        return pl.multiple_of(jnp.minimum(a8 + w * win, nrows - win), 8)

    def n_windows(blk):
        a8 = (start_ref[blk] // 8) * 8
        return jnp.maximum((start_ref[blk + 1] - a8 + win - 1) // win, 1)

    def copy(blk, w, slot):
        return pltpu.make_async_copy(y_hbm.at[pl.ds(window_start(blk, w), win)], ybuf.at[slot], sem.at[slot])

    def advance(blk, w):
        last = w + 1 >= n_windows(blk)
        return jnp.where(last, blk + 1, blk), jnp.where(last, 0, w + 1)

    nslot = ybuf.shape[0]

    @pl.when(b == 0)
    def _():
        nwin_done[0] = 0
        blk, w = jnp.int32(0), jnp.int32(0)
        for ahead in range(nslot - 1):
            @pl.when(blk < nb)
            def _():
                copy(blk, w, ahead).start()

            blk, w = advance(blk, w)

    rows = off_ref.shape[1]
    off_row = jnp.concatenate([off_ref[0, i:i + 1, :] for i in range(rows)], axis=1)
    end_row = off_row + jnp.concatenate([cnt_ref[0, i:i + 1, :] for i in range(rows)], axis=1)
    acc_ref[...] = jnp.zeros_like(acc_ref)
    nwin = n_windows(b)

    def body(w, carry):
        g = nwin_done[0]
        slot = g % nslot
        copy(b, w, slot).wait()
        blk, wn = b, w
        for _ in range(nslot - 1):
            blk, wn = advance(blk, wn)

        @pl.when(blk < nb)
        def _():
            copy(blk, wn, (g + nslot - 1) % nslot).start()

        first_new = (start_ref[b] // 8) * 8 + w * win
        r = window_start(b, w) + lax.broadcasted_iota(jnp.int32, (win, tb), 0)
        onehot_t = jnp.logical_and(jnp.logical_and(r >= off_row, r < end_row), r >= first_new)
        onehot_t = jnp.where(onehot_t, 1.0, 0.0).astype(BF16)
        y = _unpack_bf16_pairs(ybuf[slot])
        acc_ref[...] += lax.dot_general(onehot_t, y, (((0,), (0,)), ((), ())), preferred_element_type=F32)
        nwin_done[0] = g + 1
        return carry

    lax.fori_loop(0, nwin, body, 0)
    o = h_ref[...] + acc_ref[...]
    if final_norm:
        ms = jnp.mean(o * o, axis=-1, keepdims=True)
        o = o * lax.rsqrt(ms + EPS) * gf_ref[...]
    o_ref[...] = o


def _combine(h, off2d, cnt2d, blk_start, y_sorted, gf, tb, win, final_norm):
    T, D = h.shape
    nrows = y_sorted.shape[0]
    win = min(win, nrows)
    off3 = off2d.reshape(T // tb, tb // 128, 128)
    cnt3 = cnt2d.reshape(T // tb, tb // 128, 128)
    kern = functools.partial(_combine_kernel, tb=tb, win=win, nrows=nrows, final_norm=final_norm)
    grid_spec = pltpu.PrefetchScalarGridSpec(
        num_scalar_prefetch=1,
        grid=(T // tb,),
        in_specs=[pl.BlockSpec((tb, D), lambda i, s: (i, 0)),
                  pl.BlockSpec((1, tb // 128, 128), lambda i, s: (i, 0, 0)),
                  pl.BlockSpec((1, tb // 128, 128), lambda i, s: (i, 0, 0)),
                  pl.BlockSpec((1, D), lambda i, s: (0, 0)),
                  pl.BlockSpec(memory_space=pl.ANY)],
        out_specs=pl.BlockSpec((tb, D), lambda i, s: (i, 0)),
        scratch_shapes=[pltpu.VMEM((3, win, D // 2), jnp.uint32), pltpu.SemaphoreType.DMA((3,)),
                        pltpu.VMEM((tb, D), F32), pltpu.SMEM((1,), jnp.int32)],
    )
    return pl.pallas_call(
        kern, grid_spec=grid_spec, out_shape=jax.ShapeDtypeStruct((T, D), F32),
        compiler_params=_cparams(("arbitrary",)),
        name="combine",
    )(blk_start, h, off3, cnt3, gf, y_sorted)


def _ffn_kernel(x_ref, gate_ref, wg_ref, wu_ref, wd_ref, o_ref, *, fc):
    x = _unpack_bf16_pairs(x_ref[0])
    F = wg_ref.shape[2]
    acc = jnp.zeros((x.shape[0], wd_ref.shape[2]), F32)
    for c in range(F // fc):
        cols = slice(c * fc, (c + 1) * fc)
        g = jnp.dot(x, wg_ref[0, :, cols], preferred_element_type=F32)
        u = jnp.dot(x, wu_ref[0, :, cols], preferred_element_type=F32)
        hid = (g * jax.nn.sigmoid(g) * u).astype(BF16)
        acc = acc + jnp.dot(hid, wd_ref[0, cols, :], preferred_element_type=F32)
    o_ref[0] = _pack_bf16_pairs(acc * gate_ref[0])


def _ffn(xe, gate, wg, wu, wd, tm, fc):
    E, cap, Dh = xe.shape
    D = 2 * Dh
    F = wg.shape[2]
    fc = min(fc, F)
    return pl.pallas_call(
        functools.partial(_ffn_kernel, fc=fc),
        grid=(E, cap // tm),
        in_specs=[pl.BlockSpec((1, tm, Dh), lambda e, i: (e, i, 0)),
                  pl.BlockSpec((1, tm, 1), lambda e, i: (e, i, 0)),
                  pl.BlockSpec((1, D, F), lambda e, i: (e, 0, 0)),
                  pl.BlockSpec((1, D, F), lambda e, i: (e, 0, 0)),
                  pl.BlockSpec((1, F, D), lambda e, i: (e, 0, 0))],
        out_specs=pl.BlockSpec((1, tm, Dh), lambda e, i: (e, i, 0)),
        out_shape=jax.ShapeDtypeStruct((E, cap, Dh), jnp.uint32),
        compiler_params=_cparams(("parallel", "arbitrary")),
        name="expert_ffn",
    )(xe, gate, wg, wu, wd)


def _prep_layer(l, norm1, w_in, ln_v_g, ln_v_b, w_spatial, b_spatial, gla_decay_w, gla_decay_b, gla_norm,
                w_out, norm2, w_router, diff_subln):
    D = w_in.shape[1]
    w = jnp.pad(w_in[l], ((0, 0), (0, PROJ_PAD - PROJ_WIDTH))).astype(BF16)
    bs = jnp.repeat(b_spatial[l].T, B_GROUP_DIM, axis=1)
    wdec = jnp.zeros((128, 2 * C_KEY_WIDTH), F32)
    wdec = wdec.at[:C_DECAY_RANK, :C_KEY_WIDTH].set(gla_decay_w[l, 0])
    wdec = wdec.at[C_DECAY_RANK:2 * C_DECAY_RANK, C_KEY_WIDTH:].set(gla_decay_w[l, 1])
    bdec = gla_decay_b[l].reshape(1, 2 * C_KEY_WIDTH)
    wr = jnp.pad(w_router[l], ((0, 0), (0, 128 - N_EXPERTS)))
    wr_hi = wr.astype(BF16)
    wr_lo = (wr - wr_hi.astype(F32)).astype(BF16)
    return dict(
        g1=norm1[l].reshape(1, D), w=w, lng=ln_v_g[l].reshape(1, B_WIDTH), lnb=ln_v_b[l].reshape(1, B_WIDTH),
        ws=w_spatial[l].astype(BF16), bs=bs, wdec=wdec.astype(BF16), bdec=bdec,
        gn=jnp.tile(gla_norm[l], C_HEADS).reshape(1, C_WIDTH), w_out=w_out[l].astype(BF16),
        g2=norm2[l].reshape(1, D), wr_hi=wr_hi, wr_lo=wr_lo, subln=diff_subln[l].reshape(A_V_DIM, 1),
    )


def _tile(n, pref):
    t = min(n, pref)
    while n % t:
        t //= 2
    return t


def _trunk(x, layers, lam_vecs, ffn_w, norm_f):
    B, S, D = x.shape
    T = B * S
    cap = EC_CAPACITY_FACTOR * T // N_EXPERTS
    tm = _tile(T, 512)
    tk = _tile(S, 512)
    tq = min(tk, 256)
    kpos = _key_pos_lanes(tk)
    x2d = x.reshape(T, D)
    for l, p in enumerate(layers):
        lam_init = 0.8 - 0.6 * math.exp(-0.3 * l)
        aq, ak, avt, ob, cq, ck, cv, cg, laf, lab, stats = _inproj(
            x2d, p["g1"], p["w"], p["lng"], p["lnb"], p["ws"], p["bs"], p["wdec"], p["bdec"], tm, min(tk, tm), tq)
        r3 = lambda a: a.reshape(B, S, a.shape[-1])
        lq1, lk1, lq2, lk2 = (v[l:l + 1] for v in lam_vecs)
        oa = _attention(r3(aq), r3(ak), avt, stats, lq1, lk1, lq2, lk2, p["subln"], kpos, lam_init, tq, tk)
        oc = _gla(r3(cq), r3(ck), r3(cv), r3(cg), r3(laf), r3(lab), p["gn"], _tile(S, 512))
        h, hn, aff_t = _outproj(x2d, oa.reshape(T, A_WIDTH), ob, oc.reshape(T, C_WIDTH),
                                p["w_out"], p["g2"], p["wr_hi"], p["wr_lo"], tm)
        sel, pos, dst, off2d, cnt2d = _route(aff_t, cap)
        flat = lambda a: a.reshape(N_EXPERTS * T)
        idx, gate, dstl = _sc_compact(flat(sel), flat(pos), flat(dst), flat(aff_t), N_EXPERTS, T, cap)
        xe = _sc_gather_rows(hn, idx)
        wg, wu, wd = (w[l] for w in ffn_w)
        ye = _ffn(xe.reshape(N_EXPERTS, cap, D // 2), gate.reshape(N_EXPERTS, cap, 1), wg, wu, wd,
                  _tile(cap, 512), 512)
        y_sorted = _sc_scatter_rows(ye.reshape(N_EXPERTS * cap, D // 2), dstl)
        tb = _tile(T, 512)
        blk_start = jnp.concatenate([off2d.reshape(T)[::tb], jnp.full((4,), N_EXPERTS * cap, jnp.int32)])
        x2d = _combine(h, off2d, cnt2d, blk_start, y_sorted, norm_f.reshape(1, D), tb, 512,
                       final_norm=(l == len(layers) - 1))
    return x2d.reshape(B, S, D)


def kernel(x_prompt, x_sample, norm1, w_in, lam_q1, lam_k1, lam_q2, lam_k2, diff_subln, ln_v_g, ln_v_b,
           w_spatial, b_spatial, gla_decay_w, gla_decay_b, gla_norm, w_out, norm2, w_router, w_gate, w_up,
           w_down, norm_f):
    depth = w_in.shape[0]
    layers = [_prep_layer(l, norm1, w_in, ln_v_g, ln_v_b, w_spatial, b_spatial, gla_decay_w, gla_decay_b,
                          gla_norm, w_out, norm2, w_router, diff_subln) for l in range(depth)]
    ffn_w = (w_gate.astype(BF16), w_up.astype(BF16), w_down.astype(BF16))
    lam_vecs = (lam_q1, lam_k1, lam_q2, lam_k2)
    y_prompt = _trunk(x_prompt, layers, lam_vecs, ffn_w, norm_f)
    y_sample = _trunk(x_sample, layers, lam_vecs, ffn_w, norm_f)
    return (y_prompt, y_sample)
```
